```python
import math
import jax
import jax.numpy as jnp
from jax import lax
import numpy as np

D_MODEL = 1024
BATCH = 2
SEQ = 8192
DEPTH = 4
DEC_BATCH = 32
DEC_SEQ = 8
PAST_LEN = 8192
PAGE_SIZE = 128

EPS = 1e-6
N_BRANCH = 4
BRANCH_WIDTH = 512
RET_HEADS = 4
RET_DK = 64
RET_DV = 128
RET_CHUNK = 128
ROPE_BASE = 10000.0
SB_HEADS = 8
SB_DH = 64
SB_QBLOCK = 128
MOBA_HEADS = 8
MOBA_DH = 64
MOBA_BLOCK = 256
MOBA_TOPK = 3
MOBA_QBLOCK = 32
SSM_HEADS = 8
SSM_HEADDIM = 64
SSM_STATE = 64
SSM_GROUPS = 2
SSM_CONV = 4
SSM_CHUNK = 128
SSM_INNER = SSM_HEADS * SSM_HEADDIM
SSM_CONV_DIM = SSM_INNER + 2 * SSM_GROUPS * SSM_STATE
D_FF = 2816
FFN_CONV = 3
IN_WIDTHS = (RET_HEADS * RET_DK, RET_HEADS * RET_DK, RET_HEADS * RET_DV, RET_HEADS * RET_DV,
             3 * SB_HEADS * SB_DH, 3 * MOBA_HEADS * MOBA_DH,
             SSM_INNER, SSM_CONV_DIM, SSM_HEADS, N_BRANCH * D_MODEL)
D_IN = sum(IN_WIDTHS)

kernel_name = 'hybrid_ret_sb_moba_ssd_decoder_step'


def rmsnorm(x, w):
    xf = x.astype(jnp.float32)
    y = xf * lax.rsqrt(jnp.mean(xf * xf, axis=-1, keepdims=True) + EPS)
    return (y * w.astype(jnp.float32)).astype(x.dtype)


def head_rms(x):
    xf = x.astype(jnp.float32)
    return (xf * lax.rsqrt(jnp.mean(xf * xf, axis=-1, keepdims=True) + EPS)).astype(x.dtype)


def rope(x, pos):
    half = x.shape[-1] // 2
    inv = ROPE_BASE ** (-jnp.arange(half, dtype=jnp.float32) / half)
    ang = pos.astype(jnp.float32)[:, None] * inv[None, :]
    cos = jnp.cos(ang)[None, :, None, :]
    sin = jnp.sin(ang)[None, :, None, :]
    x1, x2 = x[..., :half], x[..., half:]
    return jnp.concatenate([x1 * cos - x2 * sin, x1 * sin + x2 * cos], axis=-1).astype(x.dtype)


def causal_dwconv(x, prefix, w, b):
    width, t = w.shape[0], x.shape[1]
    xp = jnp.concatenate([prefix.astype(x.dtype), x], axis=1)
    out = xp[:, 0:t] * w[0]
    for i in range(1, width):
        out = out + xp[:, i:i + t] * w[i]
    return out + b, xp[:, t:]


def retention(q, k, v, r0, log_gamma):
    bsz, t, heads, dk = q.shape
    dv = v.shape[-1]
    c = math.gcd(t, RET_CHUNK)
    nc = t // c
    qc = (q * dk ** -0.5).reshape(bsz, nc, c, heads, dk)
    kc = k.reshape(bsz, nc, c, heads, dk)
    vc = v.reshape(bsz, nc, c, heads, dv)
    i = jnp.arange(c, dtype=jnp.float32)
    diff = i[:, None] - i[None, :]
    decay = jnp.where(diff >= 0, jnp.exp(log_gamma[:, None, None] * jnp.maximum(diff, 0.0)), 0.0)
    s = jnp.einsum('bnihd,bnjhd->bnhij', qc, kc) * decay
    inner = jnp.einsum('bnhij,bnjhe->bnihe', s, vc)
    k_to_end = jnp.exp(log_gamma[:, None] * (c - 1 - i)[None, :])
    chunk_kv = jnp.einsum('bnjhd,hj,bnjhe->nbhde', kc, k_to_end, vc)
    g_chunk = jnp.exp(log_gamma * c)

    def step(r, kv):
        return g_chunk[None, :, None, None] * r + kv, r

    r_fin, r_starts = lax.scan(step, r0.astype(jnp.float32), chunk_kv)
    q_from_start = jnp.exp(log_gamma[:, None] * (i + 1.0)[None, :])
    cross = jnp.einsum('bnihd,hi,nbhde->bnihe', qc, q_from_start, r_starts)
    return (inner + cross).reshape(bsz, t, heads, dv), r_fin


def stick_breaking(q, k, v, q_pos):
    bsz, t, heads, dh = q.shape
    k_pos = jnp.arange(k.shape[1], dtype=jnp.int32)

    def block(args):
        qb, qp = args
        z = jnp.einsum('bqhd,bthd->bhqt', qb, k).astype(jnp.float32) * dh ** -0.5
        earlier = k_pos[None, :] < qp[:, None]
        log_keep = jnp.where(earlier, jax.nn.log_sigmoid(-z), 0.0)
        later_keep = lax.cumsum(log_keep, axis=3, reverse=True) - log_keep
        a = jnp.where(earlier, jnp.exp(jax.nn.log_sigmoid(z) + later_keep), 0.0)
        return jnp.einsum('bhqt,bthd->bqhd', a.astype(v.dtype), v)

    qb_len = math.gcd(t, SB_QBLOCK)
    nqb = t // qb_len
    qs = jnp.moveaxis(q.reshape(bsz, nqb, qb_len, heads, dh), 1, 0)
    ps = q_pos.reshape(nqb, qb_len)
    out = lax.map(block, (qs, ps))
    return jnp.moveaxis(out, 0, 1).reshape(bsz, t, heads, dh)


def moba(q, k, v, q_pos):
    bsz, tk, heads, dh = k.shape
    nblk = -(-tk // MOBA_BLOCK)
    pad = ((0, 0), (0, nblk * MOBA_BLOCK - tk), (0, 0), (0, 0))
    kb = jnp.pad(k, pad).reshape(bsz, nblk, MOBA_BLOCK, heads, dh).transpose(0, 3, 1, 2, 4)
    vb = jnp.pad(v, pad).reshape(bsz, nblk, MOBA_BLOCK, heads, dh).transpose(0, 3, 1, 2, 4)
    kmean = jnp.mean(kb.astype(jnp.float32), axis=3)
    n_top = min(MOBA_TOPK, nblk)
    bi = jnp.arange(bsz)[:, None, None, None]
    hi = jnp.arange(heads)[None, :, None, None]
    offs = jnp.arange(MOBA_BLOCK, dtype=jnp.int32)
    is_own_slot = jnp.arange(n_top + 1) == n_top

    def block(args):
        qb, qp = args
        nq = qb.shape[1]
        own = qp // MOBA_BLOCK
        s_blk = jnp.einsum('bqhd,bhnd->bhqn', qb.astype(jnp.float32), kmean)
        fully_past = jnp.arange(nblk, dtype=jnp.int32)[None, :] < own[:, None]
        s_blk = jnp.where(fully_past, s_blk, -jnp.inf)
        _, top = lax.top_k(s_blk, n_top)
        own_b = jnp.broadcast_to(own[None, None, :, None], (bsz, heads, nq, 1))
        idx = jnp.concatenate([top.astype(jnp.int32), own_b], axis=-1)
        valid = (idx < own[:, None]) | is_own_slot
        kg = kb[bi, hi, idx]
        vg = vb[bi, hi, idx]
        kpos = idx[..., None] * MOBA_BLOCK + offs
        mask = valid[..., None] & (kpos <= qp[None, None, :, None, None])
        s = jnp.einsum('bqhd,bhqnkd->bhqnk', qb, kg).astype(jnp.float32) * dh ** -0.5
        s = jnp.where(mask, s, -jnp.inf)
        p = jax.nn.softmax(s.reshape(bsz, heads, nq, -1), axis=-1).reshape(s.shape)
        return jnp.einsum('bhqnk,bhqnkd->bqhd', p.astype(vg.dtype), vg)

    t = q.shape[1]
    qb_len = MOBA_QBLOCK if t % MOBA_QBLOCK == 0 else 1
    nqb = t // qb_len
    qs = jnp.moveaxis(q.reshape(bsz, nqb, qb_len, heads, dh), 1, 0)
    ps = q_pos.reshape(nqb, qb_len)
    out = lax.map(block, (qs, ps))
    return jnp.moveaxis(out, 0, 1).reshape(bsz, t, heads, dh)


def ssd(x, dt, a, bm, cm, h0):
    bsz, t, heads, hd = x.shape
    rep = heads // bm.shape[2]
    bh = jnp.repeat(bm, rep, axis=2)
    ch = jnp.repeat(cm, rep, axis=2)
    c = math.gcd(t, SSM_CHUNK)
    nc = t // c
    xc = x.reshape(bsz, nc, c, heads, hd)
    bc = bh.reshape(bsz, nc, c, heads, -1)
    cc = ch.reshape(bsz, nc, c, heads, -1)
    dtc = dt.reshape(bsz, nc, c, heads)
    cum = jnp.cumsum(dtc * a, axis=2)
    cum_h = jnp.moveaxis(cum, 2, 3)
    seg = cum_h[..., :, None] - cum_h[..., None, :]
    tril = jnp.tril(jnp.ones((c, c), bool))
    decay = jnp.exp(jnp.where(tril, seg, -jnp.inf))
    scores = jnp.einsum('bnihs,bnjhs->bnhij', cc, bc) * decay
    y_diag = jnp.einsum('bnhij,bnjh,bnjhp->bnihp', scores, dtc, xc)
    to_end = jnp.exp(cum[:, :, -1:, :] - cum) * dtc
    contrib = jnp.einsum('bnjhs,bnjh,bnjhp->nbhps', bc, to_end, xc)
    chunk_decay = jnp.moveaxis(jnp.exp(cum[:, :, -1, :]), 1, 0)

    def step(h, inp):
        dec, add = inp
        return dec[..., None, None] * h + add, h

    h_fin, h_starts = lax.scan(step, h0.astype(jnp.float32), (chunk_decay, contrib))
    y_off = jnp.einsum('bnihs,bnih,nbhps->bnihp', cc, jnp.exp(cum), h_starts)
    return (y_diag + y_off).reshape(bsz, t, heads, hd), h_fin


def decoder_layer(x, pos0, past, p):
    sbk0, sbv0, mk0, mv0, r0, h0, conv0, ffn0 = past
    bsz, t, _ = x.shape
    q_pos = pos0 + jnp.arange(t, dtype=jnp.int32)
    hn = rmsnorm(x, p['norm_mix'])
    proj = hn @ p['w_in']
    split_at = np.cumsum(IN_WIDTHS)[:-1].tolist()
    rq, rk, rv, rg, sb_qkv, mb_qkv, s_z, s_xbc, s_dt, gates = jnp.split(proj, split_at, axis=-1)

    log_gamma = jnp.log1p(-jnp.exp2(-5.0 - jnp.arange(RET_HEADS, dtype=jnp.float32)))
    rq = rope(rq.reshape(bsz, t, RET_HEADS, RET_DK), q_pos)
    rk = rope(rk.reshape(bsz, t, RET_HEADS, RET_DK), q_pos)
    ret, r_new = retention(rq, rk, rv.reshape(bsz, t, RET_HEADS, RET_DV), r0, log_gamma)
    o_ret = head_rms(ret).reshape(bsz, t, -1) * jax.nn.silu(rg)

    sb = sb_qkv.reshape(bsz, t, 3, SB_HEADS, SB_DH)
    sq, sk, sv = sb[:, :, 0], sb[:, :, 1], sb[:, :, 2]
    o_sb = stick_breaking(sq, jnp.concatenate([sbk0.astype(sk.dtype), sk], axis=1),
                          jnp.concatenate([sbv0.astype(sv.dtype), sv], axis=1), q_pos).reshape(bsz, t, -1)

    mb = mb_qkv.reshape(bsz, t, 3, MOBA_HEADS, MOBA_DH)
    mq, mk, mv = mb[:, :, 0], mb[:, :, 1], mb[:, :, 2]
    o_mb = moba(mq, jnp.concatenate([mk0.astype(mk.dtype), mk], axis=1),
                jnp.concatenate([mv0.astype(mv.dtype), mv], axis=1), q_pos).reshape(bsz, t, -1)

    xbc, conv_new = causal_dwconv(s_xbc, conv0, p['ssm_conv_w'], p['ssm_conv_b'])
    xbc = jax.nn.silu(xbc)
    xs, bm, cm = jnp.split(xbc, [SSM_INNER, SSM_INNER + SSM_GROUPS * SSM_STATE], axis=-1)
    xs = xs.reshape(bsz, t, SSM_HEADS, SSM_HEADDIM)
    dt = jax.nn.softplus(s_dt.astype(jnp.float32) + p['ssm_dt_bias'].astype(jnp.float32))
    a = -jnp.exp(p['ssm_a_log'].astype(jnp.float32))
    y, h_new = ssd(xs, dt, a, bm.reshape(bsz, t, SSM_GROUPS, SSM_STATE),
                   cm.reshape(bsz, t, SSM_GROUPS, SSM_STATE), h0)
    y = y + p['ssm_d'][:, None] * xs
    o_ssm = rmsnorm(y.reshape(bsz, t, -1) * jax.nn.silu(s_z), p['ssm_norm'])

    branches = jnp.stack([o_ret, o_sb, o_mb, o_ssm], axis=2).astype(x.dtype)
    br = jnp.einsum('btiw,iwd->btid', branches, p['w_branch'])
    g = jax.nn.sigmoid(gates.reshape(bsz, t, N_BRANCH, D_MODEL))
    x1 = (x + jnp.sum(g * br, axis=2) @ p['w_out']).astype(x.dtype)

    hf = rmsnorm(x1, p['norm_ffn'])
    u, ffn_new = causal_dwconv(hf @ p['w_up'], ffn0, p['ffn_conv_w'], p['ffn_conv_b'])
    u_gate, u_up = jnp.split(u, 2, axis=-1)
    x2 = (x1 + (jax.nn.silu(u_gate) * u_up) @ p['w_down']).astype(x.dtype)
    new_state = (sk, sv, mk, mv, r_new.astype(x.dtype), h_new.astype(x.dtype),
                 conv_new.astype(x.dtype), ffn_new.astype(x.dtype))
    return x2, new_state


def run_trunk(x, pos0, past_fn, layer_params, norm_final):
    per_layer = []
    for l in range(DEPTH):
        x, st = decoder_layer(x, pos0, past_fn(l), layer_params(l))
        per_layer.append(st)
    new_state = [jnp.stack(parts) for parts in zip(*per_layer)]
    return rmsnorm(x, norm_final), new_state


def setup_inputs(seed: int = 0) -> dict:
    key = jax.random.key(seed)
    ks = jax.random.split(key, 32)
    f32 = jnp.float32
    n_pages = PAST_LEN // PAGE_SIZE
    n_pool = (5 * DEC_BATCH * n_pages) // 4

    def nrm(i, shape, scale=1.0):
        return jax.random.normal(ks[i], shape, f32) * scale

    sb_shape = (DEPTH, n_pool, PAGE_SIZE, SB_HEADS, SB_DH)
    mb_shape = (DEPTH, n_pool, PAGE_SIZE, MOBA_HEADS, MOBA_DH)
    page_table = jax.random.permutation(ks[6], n_pool)[: DEC_BATCH * n_pages].reshape(DEC_BATCH, n_pages).astype(jnp.int32)
    dt = jnp.exp(jax.random.uniform(ks[15], (DEPTH, SSM_HEADS), f32, math.log(1e-3), math.log(1e-1)))
    return {
        'x_prompt': nrm(0, (BATCH, SEQ, D_MODEL)),
        'x_sample': nrm(1, (DEC_BATCH, DEC_SEQ, D_MODEL)),
        'cache_sb_k': nrm(2, sb_shape),
        'cache_sb_v': nrm(3, sb_shape),
        'cache_moba_k': nrm(4, mb_shape),
        'cache_moba_v': nrm(5, mb_shape),
        'page_table': page_table,
        'state_ret': nrm(7, (DEPTH, DEC_BATCH, RET_HEADS, RET_DK, RET_DV)),
        'state_ssm': nrm(8, (DEPTH, DEC_BATCH, SSM_HEADS, SSM_HEADDIM, SSM_STATE), 0.5),
        'state_ssm_conv': nrm(9, (DEPTH, DEC_BATCH, SSM_CONV - 1, SSM_CONV_DIM)),
        'state_ffn_conv': nrm(10, (DEPTH, DEC_BATCH, FFN_CONV - 1, 2 * D_FF)),
        'norm_mix': 1.0 + nrm(11, (DEPTH, D_MODEL), 0.02),
        'w_in': nrm(12, (DEPTH, D_MODEL, D_IN), D_MODEL ** -0.5),
        'ssm_conv_w': nrm(13, (DEPTH, SSM_CONV, SSM_CONV_DIM), SSM_CONV ** -0.5),
        'ssm_conv_b': nrm(14, (DEPTH, SSM_CONV_DIM), 0.02),
        'ssm_dt_bias': dt + jnp.log(-jnp.expm1(-dt)),
        'ssm_a_log': jnp.log(jax.random.uniform(ks[16], (DEPTH, SSM_HEADS), f32, 1.0, 16.0)),
        'ssm_d': 1.0 + nrm(17, (DEPTH, SSM_HEADS), 0.1),
        'ssm_norm': 1.0 + nrm(18, (DEPTH, SSM_INNER), 0.02),
        'w_branch': nrm(19, (DEPTH, N_BRANCH, BRANCH_WIDTH, D_MODEL), BRANCH_WIDTH ** -0.5),
        'w_out': nrm(20, (DEPTH, D_MODEL, D_MODEL), D_MODEL ** -0.5),
        'norm_ffn': 1.0 + nrm(21, (DEPTH, D_MODEL), 0.02),
        'w_up': nrm(22, (DEPTH, D_MODEL, 2 * D_FF), D_MODEL ** -0.5),
        'ffn_conv_w': nrm(23, (DEPTH, FFN_CONV, 2 * D_FF), FFN_CONV ** -0.5),
        'ffn_conv_b': nrm(24, (DEPTH, 2 * D_FF), 0.02),
        'w_down': nrm(25, (DEPTH, D_FF, D_MODEL), D_FF ** -0.5),
        'norm_final': 1.0 + nrm(26, (D_MODEL,), 0.02),
    }


def reference(x_prompt, x_sample, cache_sb_k, cache_sb_v, cache_moba_k, cache_moba_v, page_table,
              state_ret, state_ssm, state_ssm_conv, state_ffn_conv,
              norm_mix, w_in, ssm_conv_w, ssm_conv_b, ssm_dt_bias, ssm_a_log, ssm_d, ssm_norm,
              w_branch, w_out, norm_ffn, w_up, ffn_conv_w, ffn_conv_b, w_down, norm_final):
    weights = {'norm_mix': norm_mix, 'w_in': w_in, 'ssm_conv_w': ssm_conv_w, 'ssm_conv_b': ssm_conv_b,
               'ssm_dt_bias': ssm_dt_bias, 'ssm_a_log': ssm_a_log, 'ssm_d': ssm_d, 'ssm_norm': ssm_norm,
               'w_branch': w_branch, 'w_out': w_out, 'norm_ffn': norm_ffn, 'w_up': w_up,
               'ffn_conv_w': ffn_conv_w, 'ffn_conv_b': ffn_conv_b, 'w_down': w_down}

    def layer_params(l):
        return {name: w[l] for name, w in weights.items()}

    def prompt_past(l):
        b, dt = x_prompt.shape[0], x_prompt.dtype
        return (jnp.zeros((b, 0, SB_HEADS, SB_DH), dt), jnp.zeros((b, 0, SB_HEADS, SB_DH), dt),
                jnp.zeros((b, 0, MOBA_HEADS, MOBA_DH), dt), jnp.zeros((b, 0, MOBA_HEADS, MOBA_DH), dt),
                jnp.zeros((b, RET_HEADS, RET_DK, RET_DV), dt),
                jnp.zeros((b, SSM_HEADS, SSM_HEADDIM, SSM_STATE), dt),
                jnp.zeros((b, SSM_CONV - 1, SSM_CONV_DIM), dt),
                jnp.zeros((b, FFN_CONV - 1, 2 * D_FF), dt))

    def sample_past(l):
        def gather(cache):
            pages = cache[l, page_table]
            return pages.reshape(page_table.shape[0], -1, pages.shape[-2], pages.shape[-1])
        return (gather(cache_sb_k), gather(cache_sb_v), gather(cache_moba_k), gather(cache_moba_v),
                state_ret[l], state_ssm[l], state_ssm_conv[l], state_ffn_conv[l])

    y_prompt, st_p = run_trunk(x_prompt, 0, prompt_past, layer_params, norm_final)
    y_sample, st_s = run_trunk(x_sample, page_table.shape[1] * PAGE_SIZE, sample_past, layer_params, norm_final)
    sbk_p, sbv_p, mk_p, mv_p, ret_p, ssm_p, conv_p, ffn_p = st_p
    sbk_s, sbv_s, mk_s, mv_s, ret_s, ssm_s, conv_s, ffn_s = st_s
    return (y_prompt, y_sample, sbk_p, sbv_p, mk_p, mv_p, ret_p, ssm_p, conv_p, ffn_p,
            sbk_s, sbv_s, mk_s, mv_s, ret_s, ssm_s, conv_s, ffn_s)
```

```python
import functools
import math

import numpy as np
import jax
import jax.numpy as jnp
from jax import lax
from jax.experimental import pallas as pl
from jax.experimental.pallas import tpu as pltpu

F32 = jnp.float32
BF16 = jnp.bfloat16

EPS = 1e-6
D_MODEL = 1024
BRANCH_WIDTH = 512
RET_HEADS, RET_DK, RET_DV = 4, 64, 128
ROPE_BASE = 10000.0
ATT_HEADS, ATT_DH = 8, 64
MOBA_BLOCK, MOBA_TOPK = 256, 3
SSM_HEADS, SSM_HEADDIM, SSM_STATE, SSM_GROUPS, SSM_CONV = 8, 64, 64, 2, 4
SSM_INNER = SSM_HEADS * SSM_HEADDIM
SSM_CONV_DIM = SSM_INNER + 2 * SSM_GROUPS * SSM_STATE
D_FF = 2816
FFN_CONV = 3
PAGE_SIZE = 128
CHUNK = 128
ATT_TILE = 256
LOG_GAMMA = tuple(math.log1p(-2.0 ** (-5.0 - h)) for h in range(RET_HEADS))

OFF_RQ, OFF_RK, OFF_RV, OFF_RG = 0, 256, 512, 1024
OFF_SBQ, OFF_SBK, OFF_SBV = 1536, 2048, 2560
OFF_MBQ, OFF_MBK, OFF_MBV = 3072, 3584, 4096
OFF_Z, OFF_DT, OFF_XBC, OFF_GATE = 4608, 5120, 5376, 6144
DT_PAD = 256
P_TOTAL = OFF_GATE + 4 * D_MODEL
NEG_BIAS = -1e30

LANE = 128
SUBLANE = 8
VMEM_LIMIT = 56 * 1024 * 1024

NT_DIMS = (((1,), (1,)), ((), ()))
TN_DIMS = (((0,), (0,)), ((), ()))


def _dot(a, b):
    return jnp.dot(a, b, preferred_element_type=F32)


def _dot_nt(a, b):
    return lax.dot_general(a, b, NT_DIMS, preferred_element_type=F32)


def _dot_tn(a, b):
    return lax.dot_general(a, b, TN_DIMS, preferred_element_type=F32)


def _split_bf16(x, n):
    terms, r = [], x
    for _ in range(n):
        t = r.astype(BF16)
        terms.append(t)
        r = r - t.astype(F32)
    return terms


def _dot_split_rhs(a01, x, n=3):
    out = None
    for t in _split_bf16(x, n):
        d = _dot(a01, t)
        out = d if out is None else out + d
    return out


def _dot_split_lhs(x, a01, n=3):
    out = None
    for t in _split_bf16(x, n):
        d = _dot(t, a01)
        out = d if out is None else out + d
    return out


def _iota(shape, dim):
    return lax.broadcasted_iota(jnp.int32, shape, dim)


def _sigmoid(x):
    return 1.0 / (1.0 + jnp.exp(-x))


def _silu(x):
    return x * _sigmoid(x)


def _log_sigmoid(x):
    return jnp.minimum(x, 0.0) - jnp.log1p(jnp.exp(-jnp.abs(x)))


def _softplus(x):
    return jnp.maximum(x, 0.0) + jnp.log1p(jnp.exp(-jnp.abs(x)))


def _pad_rows(x, rows):
    if x.shape[0] == rows:
        return x
    return jnp.concatenate([x, jnp.zeros((rows - x.shape[0],) + x.shape[1:], x.dtype)], axis=0)


def _tile(n, cap, mult=SUBLANE):
    if n <= cap:
        return n
    for t in range(cap - cap % mult, 0, -mult):
        if n % t == 0:
            return t
    raise ValueError(f"no tile for {n}")


def _params(*sem):
    return pltpu.CompilerParams(dimension_semantics=sem, vmem_limit_bytes=VMEM_LIMIT)


def _rmsnorm_kernel(x_ref, w_ref, o_ref):
    x = x_ref[...]
    ms = jnp.mean(x * x, axis=-1, keepdims=True)
    o_ref[...] = (x * lax.rsqrt(ms + EPS) * w_ref[...]).astype(o_ref.dtype)


def _rmsnorm(x, w, out_dtype):
    n, d = x.shape
    tm = _tile(n, 512)
    return pl.pallas_call(
        _rmsnorm_kernel,
        grid=(n // tm,),
        in_specs=[pl.BlockSpec((tm, d), lambda i: (i, 0)), pl.BlockSpec((1, d), lambda i: (0, 0))],
        out_specs=pl.BlockSpec((tm, d), lambda i: (i, 0)),
        out_shape=jax.ShapeDtypeStruct((n, d), out_dtype),
        compiler_params=_params("parallel"),
    )(x, w.reshape(1, d))


def _matmul_kernel(x_ref, w_ref, o_ref):
    o_ref[...] = _dot(x_ref[...], w_ref[...])


def _matmul(x, w):
    n, k = x.shape
    m = w.shape[1]
    tm = _tile(n, 1024)
    tn = _tile(m, 1536, LANE)
    return pl.pallas_call(
        _matmul_kernel,
        grid=(m // tn, n // tm),
        in_specs=[pl.BlockSpec((tm, k), lambda c, r: (r, 0)), pl.BlockSpec((k, tn), lambda c, r: (0, c))],
        out_specs=pl.BlockSpec((tm, tn), lambda c, r: (r, c)),
        out_shape=jax.ShapeDtypeStruct((n, m), F32),
        compiler_params=_params("parallel", "parallel"),
    )(x, w)


def _retention_kernel(q_ref, k_ref, v_ref, g_ref, cos_ref, sin_ref, r0_ref, o_ref, rn_ref, r_ref, *, c, tv):
    n = pl.program_id(1)
    hk, hv = RET_HEADS * RET_DK, RET_HEADS * RET_DV

    @pl.when(n == 0)
    def _():
        r_ref[...] = jnp.zeros((hk, hv), F32)
        for h in range(RET_HEADS):
            r_ref[RET_DK * h:RET_DK * (h + 1), RET_DV * h:RET_DV * (h + 1)] = r0_ref[h]

    rows = q_ref.shape[0]
    q, k, v = _pad_rows(q_ref[...], c), _pad_rows(k_ref[...], c), _pad_rows(v_ref[...], c)
    cos, sin = _pad_rows(cos_ref[...], c), _pad_rows(sin_ref[...], c)
    lane = _iota((c, hk), 1)
    head_of_lane = lane >> 6
    first_half = (_iota((c, LANE), 1) & 63) < 32

    def rope(x):
        parts = []
        for s in range(hk // LANE):
            xs = x[:, LANE * s:LANE * (s + 1)]
            parts.append(jnp.where(first_half, pltpu.roll(xs, LANE - 32, 1), pltpu.roll(xs, 32, 1)))
        return x * cos + jnp.concatenate(parts, axis=1) * sin

    qr = rope(q) * (RET_DK ** -0.5)
    kr = rope(k)
    lg = jnp.full((c, hk), LOG_GAMMA[RET_HEADS - 1], F32)
    for h in range(RET_HEADS - 2, -1, -1):
        lg = jnp.where(head_of_lane == h, LOG_GAMMA[h], lg)
    ri = _iota((c, hk), 0).astype(F32)
    q_from_start = jnp.exp(lg * (ri + 1.0))
    k_to_end = jnp.exp(lg * (float(tv - 1) - ri))
    kb, vb = kr.astype(BF16), v.astype(BF16)
    di = _iota((c, c), 0) - _iota((c, c), 1)
    dif = jnp.maximum(di, 0).astype(F32)
    inner = []
    for h in range(RET_HEADS):
        decay = jnp.where(di >= 0, jnp.exp(LOG_GAMMA[h] * dif), 0.0)
        qm = jnp.where(head_of_lane == h, qr, 0.0).astype(BF16)
        s = _dot_nt(qm, kb)
        inner.append(_dot((s * decay).astype(BF16), vb[:, RET_DV * h:RET_DV * (h + 1)]))
    r = r_ref[...]
    cross = _dot((qr * q_from_start).astype(BF16), r.astype(BF16))
    ret = jnp.concatenate(inner, axis=1) + cross
    kv = _dot_tn((kr * k_to_end).astype(BF16), vb)
    row_head = _iota((hk, hv), 0) >> 6
    col_head = _iota((hk, hv), 1) >> 7
    g_chunk = jnp.full((hk, hv), math.exp(LOG_GAMMA[RET_HEADS - 1] * tv), F32)
    for h in range(RET_HEADS - 2, -1, -1):
        g_chunk = jnp.where(row_head == h, math.exp(LOG_GAMMA[h] * tv), g_chunk)
    r_new = g_chunk * r + jnp.where(row_head == col_head, kv, 0.0)
    r_ref[...] = r_new
    g = _pad_rows(g_ref[...], c)
    outs = []
    for h in range(RET_HEADS):
        x = ret[:, RET_DV * h:RET_DV * (h + 1)]
        y = x * lax.rsqrt(jnp.mean(x * x, axis=-1, keepdims=True) + EPS)
        outs.append(y * _silu(g[:, RET_DV * h:RET_DV * (h + 1)]))
    o_ref[...] = jnp.concatenate(outs, axis=1)[:rows]

    @pl.when(n == pl.num_programs(1) - 1)
    def _():
        for h in range(RET_HEADS):
            rn_ref[h] = r_new[RET_DK * h:RET_DK * (h + 1), RET_DV * h:RET_DV * (h + 1)]


def _retention(proj, cos, sin, r0):
    b, t, _ = proj.shape
    cb = min(t, CHUNK)
    hk, hv = RET_HEADS * RET_DK, RET_HEADS * RET_DV
    kern = functools.partial(_retention_kernel, c=CHUNK, tv=cb)
    return pl.pallas_call(
        kern,
        grid=(b, t // cb),
        in_specs=[
            pl.BlockSpec((None, cb, hk), lambda i, n: (i, n, OFF_RQ // hk)),
            pl.BlockSpec((None, cb, hk), lambda i, n: (i, n, OFF_RK // hk)),
            pl.BlockSpec((None, cb, hv), lambda i, n: (i, n, OFF_RV // hv)),
            pl.BlockSpec((None, cb, hv), lambda i, n: (i, n, OFF_RG // hv)),
            pl.BlockSpec((cb, hk), lambda i, n: (n, 0)),
            pl.BlockSpec((cb, hk), lambda i, n: (n, 0)),
            pl.BlockSpec((None, RET_HEADS, RET_DK, RET_DV), lambda i, n: (i, 0, 0, 0)),
        ],
        out_specs=[
            pl.BlockSpec((None, cb, hv), lambda i, n: (i, n, 0)),
            pl.BlockSpec((None, RET_HEADS, RET_DK, RET_DV), lambda i, n: (i, 0, 0, 0)),
        ],
        out_shape=[jax.ShapeDtypeStruct((b, t, hv), F32), jax.ShapeDtypeStruct(r0.shape, F32)],
        scratch_shapes=[pltpu.VMEM((hk, hv), F32)],
        compiler_params=_params("parallel", "arbitrary"),
    )(proj, proj, proj, proj, cos, sin, r0)


def _ssd_kernel(z_ref, dt_ref, xbc_ref, conv0_ref, h0_ref, cw_ref, cb_ref, dtb_ref, alog_ref, dx_ref, nw_ref,
                o_ref, convn_ref, hn_ref, xp_ref, h_ref, *, c, tv):
    n = pl.program_id(1)
    rows = xbc_ref.shape[0]
    half = SSM_INNER // SSM_GROUPS

    @pl.when(n == 0)
    def _():
        xp_ref[0:SUBLANE, :] = conv0_ref[...]
        h_ref[...] = h0_ref[...]

    xraw = _pad_rows(xbc_ref[...], c)
    xp_ref[SUBLANE:SUBLANE + c, :] = xraw
    cw = cw_ref[...]
    acc = xraw * cw[SSM_CONV - 1:SSM_CONV] + cb_ref[...]
    for i in range(SSM_CONV - 1):
        off = SUBLANE - (SSM_CONV - 1) + i
        acc = acc + xp_ref[off:off + c, :] * cw[i:i + 1]
    tail = xp_ref[tv:tv + SUBLANE, :]
    xp_ref[0:SUBLANE, :] = tail
    xbc = _silu(acc)
    xs = xbc[:, :SSM_INNER]
    bm = xbc[:, SSM_INNER:SSM_INNER + LANE]
    cm = xbc[:, SSM_INNER + LANE:SSM_INNER + 2 * LANE]

    dt = _softplus(_pad_rows(dt_ref[...], c) + dtb_ref[...])
    if tv < c:
        dt = jnp.where(_iota((c, LANE), 0) < tv, dt, 0.0)
    da = dt * (-jnp.exp(alog_ref[...]))
    tril = _iota((c, c), 0) >= _iota((c, c), 1)
    cum = _dot_split_rhs(tril.astype(BF16), da)
    cum_t = cum.T
    expand = ((_iota((LANE, SSM_INNER), 1) >> 6) == _iota((LANE, SSM_INNER), 0)).astype(BF16)
    cumx = _dot_split_lhs(cum, expand)
    dtx = _dot_split_lhs(dt, expand)
    to_end = jnp.exp(cumx[tv - 1:tv, :] - cumx) * dtx
    x_to_end = (xs * to_end).astype(BF16)
    x_dt = (xs * dtx).astype(BF16)
    bb = bm.astype(BF16)
    lane = _iota((c, LANE), 1)
    cg = [jnp.where((lane >> 6) == g, cm, 0.0).astype(BF16) for g in range(SSM_GROUPS)]
    scores = [_dot_nt(cg[g], bb) for g in range(SSM_GROUPS)]
    y_pairs = []
    for p in range(SSM_HEADS // 2):
        y_head = []
        for hh in range(2):
            h = 2 * p + hh
            seg = cum[:, h:h + 1] - cum_t[h:h + 1, :]
            m = scores[h // (SSM_HEADS // SSM_GROUPS)] * jnp.exp(jnp.where(tril, seg, -jnp.inf))
            y_head.append(_dot(m.astype(BF16), x_dt[:, LANE * p:LANE * (p + 1)]))
        y_pairs.append(jnp.where(lane < SSM_HEADDIM, y_head[0], y_head[1]))
    hs = h_ref[...]
    hsb = hs.astype(BF16)
    y_off = jnp.concatenate([_dot_nt(cg[g], hsb[half * g:half * (g + 1)]) for g in range(SSM_GROUPS)], axis=1)
    y = jnp.concatenate(y_pairs, axis=1) + y_off * jnp.exp(cumx) + dx_ref[...] * xs
    tz = y * _silu(_pad_rows(z_ref[...], c))
    out = tz * lax.rsqrt(jnp.mean(tz * tz, axis=-1, keepdims=True) + EPS) * nw_ref[...]
    o_ref[...] = out[:rows]

    expand_t = ((_iota((SSM_INNER, LANE), 0) >> 6) == _iota((SSM_INNER, LANE), 1)).astype(BF16)
    cum_last = jnp.broadcast_to(cum_t[:, tv - 1:tv], (LANE, LANE))
    chunk_decay = jnp.exp(_dot_split_rhs(expand_t, cum_last))
    contrib = jnp.concatenate([_dot_tn(x_to_end[:, half * g:half * (g + 1)], bb) for g in range(SSM_GROUPS)], axis=0)
    h_new = chunk_decay * hs + contrib
    h_ref[...] = h_new

    @pl.when(n == pl.num_programs(1) - 1)
    def _():
        hn_ref[...] = h_new
        convn_ref[...] = tail


def _ssd(proj, conv0, h0, lp):
    b, t, _ = proj.shape
    cb = min(t, CHUNK)
    kern = functools.partial(_ssd_kernel, c=CHUNK, tv=cb)
    const = lambda i, n: (0, 0)
    return pl.pallas_call(
        kern,
        grid=(b, t // cb),
        in_specs=[
            pl.BlockSpec((None, cb, SSM_INNER), lambda i, n: (i, n, OFF_Z // SSM_INNER)),
            pl.BlockSpec((None, cb, LANE), lambda i, n: (i, n, OFF_DT // LANE)),
            pl.BlockSpec((None, cb, SSM_CONV_DIM), lambda i, n: (i, n, OFF_XBC // SSM_CONV_DIM)),
            pl.BlockSpec((None, SUBLANE, SSM_CONV_DIM), lambda i, n: (i, 0, 0)),
            pl.BlockSpec((None, SSM_INNER, LANE), lambda i, n: (i, 0, 0)),
            pl.BlockSpec((SSM_CONV, SSM_CONV_DIM), const),
            pl.BlockSpec((1, SSM_CONV_DIM), const),
            pl.BlockSpec((1, LANE), const),
            pl.BlockSpec((1, LANE), const),
            pl.BlockSpec((1, SSM_INNER), const),
            pl.BlockSpec((1, SSM_INNER), const),
        ],
        out_specs=[
            pl.BlockSpec((None, cb, SSM_INNER), lambda i, n: (i, n, 0)),
            pl.BlockSpec((None, SUBLANE, SSM_CONV_DIM), lambda i, n: (i, 0, 0)),
            pl.BlockSpec((None, SSM_INNER, LANE), lambda i, n: (i, 0, 0)),
        ],
        out_shape=[
            jax.ShapeDtypeStruct((b, t, SSM_INNER), F32),
            jax.ShapeDtypeStruct((b, SUBLANE, SSM_CONV_DIM), F32),
            jax.ShapeDtypeStruct((b, SSM_INNER, LANE), F32),
        ],
        scratch_shapes=[pltpu.VMEM((CHUNK + SUBLANE, SSM_CONV_DIM), F32), pltpu.VMEM((SSM_INNER, LANE), F32)],
        compiler_params=_params("parallel", "arbitrary"),
    )(proj, proj, proj, conv0, h0, lp["ssm_conv_w"], lp["ssm_conv_b"], lp["ssm_dt_bias"], lp["ssm_a_log"],
      lp["ssm_d"], lp["ssm_norm"])


def _sb_tile(z, earlier, carry, upper):
    ls = _log_sigmoid(z)
    lk = ls - z
    if earlier is not None:
        lk = jnp.where(earlier, lk, 0.0)
    within = _dot_split_lhs(lk, upper, 2)
    a = jnp.exp(ls + within + carry)
    if earlier is not None:
        a = jnp.where(earlier, a, 0.0)
    return a, carry + jnp.sum(lk, axis=1, keepdims=True)


def _upper_ones(tk):
    return (_iota((tk, tk), 0) > _iota((tk, tk), 1)).astype(BF16)


def _sb_prompt_kernel(it_ref, jt_ref, q_ref, k_ref, v_ref, o_ref, acc_ref, carry_ref):
    s = pl.program_id(2)
    i, j = it_ref[s], jt_ref[s]
    tq, tk = q_ref.shape[0], k_ref.shape[0]

    @pl.when(j == i)
    def _():
        acc_ref[...] = jnp.zeros(acc_ref.shape, F32)
        carry_ref[...] = jnp.zeros(carry_ref.shape, F32)

    q2 = q_ref[...] * (ATT_DH ** -0.5)
    k2 = k_ref[...].astype(BF16)
    v2 = v_ref[...].astype(BF16)
    lane = _iota((tq, LANE), 1)
    earlier = (j * tk + _iota((tq, tk), 1)) < (i * tq + _iota((tq, tk), 0))
    upper = _upper_ones(tk)
    for hh in range(2):
        qm = jnp.where((lane >= ATT_DH) == (hh == 1), q2, 0.0).astype(BF16)
        z = _dot_nt(qm, k2)
        a, carry = _sb_tile(z, earlier, carry_ref[hh], upper)
        acc_ref[hh] += _dot(a.astype(BF16), v2)
        carry_ref[hh] = carry

    @pl.when(j == 0)
    def _():
        o_ref[...] = jnp.where(lane < ATT_DH, acc_ref[0], acc_ref[1])


def _tri_tables(nq):
    it, jt = [], []
    for i in range(nq):
        for j in range(i, -1, -1):
            it.append(i)
            jt.append(j)
    return jnp.asarray(np.array(it, np.int32)), jnp.asarray(np.array(jt, np.int32))


def _sb_prompt(proj):
    b, t, _ = proj.shape
    tq = min(ATT_TILE, t)
    it, jt = _tri_tables(t // tq)
    pairs = ATT_HEADS // 2
    grid_spec = pltpu.PrefetchScalarGridSpec(
        num_scalar_prefetch=2,
        grid=(b, pairs, int(it.shape[0])),
        in_specs=[
            pl.BlockSpec((None, tq, LANE), lambda i, p, s, it, jt: (i, it[s], OFF_SBQ // LANE + p)),
            pl.BlockSpec((None, tq, LANE), lambda i, p, s, it, jt: (i, jt[s], OFF_SBK // LANE + p)),
            pl.BlockSpec((None, tq, LANE), lambda i, p, s, it, jt: (i, jt[s], OFF_SBV // LANE + p)),
        ],
        out_specs=pl.BlockSpec((None, tq, LANE), lambda i, p, s, it, jt: (i, it[s], p)),
        scratch_shapes=[pltpu.VMEM((2, tq, LANE), F32), pltpu.VMEM((2, tq, 1), F32)],
    )
    return pl.pallas_call(
        _sb_prompt_kernel,
        grid_spec=grid_spec,
        out_shape=jax.ShapeDtypeStruct((b, t, ATT_HEADS * ATT_DH), F32),
        compiler_params=_params("parallel", "parallel", "arbitrary"),
    )(it, jt, proj, proj, proj)


def _head_rows(ref, h):
    return ref[pl.ds(h, PAGE_SIZE, stride=ATT_HEADS), :]


def _sb_sample_kernel(pt_ref, l_ref, q_ref, kn_ref, vn_ref, kp_ref, vp_ref, o_ref, acc_ref, carry_ref):
    jj = pl.program_id(1)
    nq = q_ref.shape[0] // ATT_HEADS
    upper = _upper_ones(PAGE_SIZE)
    q = (q_ref[...] * (ATT_DH ** -0.5)).astype(BF16)

    def visit(get_k, get_v, earlier):
        z = jnp.concatenate(
            [_dot_nt(q[nq * h:nq * (h + 1)], get_k(h).astype(BF16)) for h in range(ATT_HEADS)], axis=0)
        a, carry = _sb_tile(z, earlier, carry_ref[...], upper)
        ab = a.astype(BF16)
        acc_ref[...] += jnp.concatenate(
            [_dot(ab[nq * h:nq * (h + 1)], get_v(h).astype(BF16)) for h in range(ATT_HEADS)], axis=0)
        carry_ref[...] = carry

    @pl.when(jj == 0)
    def _():
        acc_ref[...] = jnp.zeros(acc_ref.shape, F32)
        carry_ref[...] = jnp.zeros(carry_ref.shape, F32)
        rows = _iota((ATT_HEADS * nq, PAGE_SIZE), 0) & (nq - 1)
        earlier = _iota((ATT_HEADS * nq, PAGE_SIZE), 1) < rows
        visit(lambda h: kn_ref[h], lambda h: vn_ref[h], earlier)

    visit(lambda h: _head_rows(kp_ref, h), lambda h: _head_rows(vp_ref, h), None)

    @pl.when(jj == pl.num_programs(1) - 1)
    def _():
        o_ref[...] = acc_ref[...]


def _sb_sample(q, k_new, v_new, cache_k, cache_v, page_table, layer):
    b, rows, _ = q.shape
    n_pages = page_table.shape[1]
    page_rows = cache_k.shape[2]
    page = lambda i, jj, pt, l: (l[0], pt[i, n_pages - 1 - jj], 0, 0)
    per_b3 = lambda i, jj, pt, l: (i, 0, 0)
    per_b4 = lambda i, jj, pt, l: (i, 0, 0, 0)
    grid_spec = pltpu.PrefetchScalarGridSpec(
        num_scalar_prefetch=2,
        grid=(b, n_pages),
        in_specs=[
            pl.BlockSpec((None, rows, ATT_DH), per_b3),
            pl.BlockSpec((None, ATT_HEADS, PAGE_SIZE, ATT_DH), per_b4),
            pl.BlockSpec((None, ATT_HEADS, PAGE_SIZE, ATT_DH), per_b4),
            pl.BlockSpec((None, None, page_rows, ATT_DH), page),
            pl.BlockSpec((None, None, page_rows, ATT_DH), page),
        ],
        out_specs=pl.BlockSpec((None, rows, ATT_DH), per_b3),
        scratch_shapes=[pltpu.VMEM((rows, ATT_DH), F32), pltpu.VMEM((rows, 1), F32)],
    )
    return pl.pallas_call(
        _sb_sample_kernel,
        grid_spec=grid_spec,
        out_shape=jax.ShapeDtypeStruct((b, rows, ATT_DH), F32),
        compiler_params=_params("parallel", "arbitrary"),
    )(page_table, layer, q, k_new, v_new, cache_k, cache_v)


def _select_bias(scores, own):
    blk = _iota(scores.shape, 1)
    s = jnp.where(blk < own, scores, -jnp.inf)
    keep = blk == own
    for _ in range(MOBA_TOPK):
        m = jnp.max(s, axis=1, keepdims=True)
        is_max = (s == m) & (m > -jnp.inf)
        first = jnp.min(jnp.where(is_max, blk, LANE), axis=1, keepdims=True)
        pick = blk == first
        keep = keep | pick
        s = jnp.where(pick, -jnp.inf, s)
    return jnp.where(keep, 0.0, NEG_BIAS)


def _kmean_prompt_kernel(k_ref, o_ref):
    o_ref[...] = jnp.mean(k_ref[...], axis=0, keepdims=True)


def _kmean_prompt(proj):
    b, t, _ = proj.shape
    nb = t // MOBA_BLOCK
    w = ATT_HEADS * ATT_DH
    return pl.pallas_call(
        _kmean_prompt_kernel,
        grid=(b, nb),
        in_specs=[pl.BlockSpec((None, MOBA_BLOCK, w), lambda i, n: (i, n, OFF_MBK // w))],
        out_specs=pl.BlockSpec((None, None, 1, w), lambda i, n: (i, n, 0, 0)),
        out_shape=jax.ShapeDtypeStruct((b, nb, 1, w), F32),
        compiler_params=_params("parallel", "parallel"),
    )(proj)


def _select_prompt_kernel(q_ref, km_ref, o_ref):
    i = pl.program_id(2)
    tq = q_ref.shape[0]
    q = q_ref[...]
    km = km_ref[...]
    lane = _iota((tq, LANE), 1)
    own = (i * tq + _iota((tq, 1), 0)) >> 8
    for hh in range(2):
        qm = jnp.where((lane >= ATT_DH) == (hh == 1), q, 0.0)
        scores = lax.dot_general(qm, km, NT_DIMS, precision=lax.Precision.HIGHEST, preferred_element_type=F32)
        o_ref[hh] = _select_bias(scores, own).astype(BF16)


def _select_prompt(proj, kmean):
    b, t, _ = proj.shape
    tq = min(ATT_TILE, t)
    return pl.pallas_call(
        _select_prompt_kernel,
        grid=(b, ATT_HEADS // 2, t // tq),
        in_specs=[
            pl.BlockSpec((None, tq, LANE), lambda i, p, n: (i, n, OFF_MBQ // LANE + p)),
            pl.BlockSpec((None, LANE, LANE), lambda i, p, n: (i, 0, p)),
        ],
        out_specs=pl.BlockSpec((None, 2, tq, LANE), lambda i, p, n: (i, p, n, 0)),
        out_shape=jax.ShapeDtypeStruct((b, ATT_HEADS, t, LANE), BF16),
        compiler_params=_params("parallel", "parallel", "parallel"),
    )(proj, kmean)


def _softmax_step(s, v, m_prev, l_prev, acc_prev):
    m_new = jnp.maximum(m_prev, jnp.max(s, axis=1, keepdims=True))
    p = jnp.exp(s - m_new)
    alpha = jnp.exp(m_prev - m_new)
    return m_new, alpha * l_prev + jnp.sum(p, axis=1, keepdims=True), alpha * acc_prev + _dot(p.astype(BF16), v)


def _moba_prompt_kernel(it_ref, jt_ref, q_ref, k_ref, v_ref, bias_ref, o_ref, acc_ref, m_ref, l_ref):
    s_id = pl.program_id(2)
    i, j = it_ref[s_id], jt_ref[s_id]
    tq, tk = q_ref.shape[0], k_ref.shape[0]

    @pl.when(j == i)
    def _():
        acc_ref[...] = jnp.zeros(acc_ref.shape, F32)
        m_ref[...] = jnp.full(m_ref.shape, -jnp.inf, F32)
        l_ref[...] = jnp.zeros(l_ref.shape, F32)

    q2 = q_ref[...] * (ATT_DH ** -0.5)
    lane = _iota((tq, LANE), 1)
    onehot = (_iota((tk, LANE), 1) == j).astype(BF16)
    k_aug = jnp.concatenate([k_ref[...].astype(BF16), onehot], axis=1)
    v2 = v_ref[...].astype(BF16)
    visible = (j * tk + _iota((tq, tk), 1)) <= (i * tq + _iota((tq, tk), 0))
    for hh in range(2):
        qm = jnp.where((lane >= ATT_DH) == (hh == 1), q2, 0.0).astype(BF16)
        q_aug = jnp.concatenate([qm, bias_ref[hh]], axis=1)
        s = jnp.where(visible, _dot_nt(q_aug, k_aug), -jnp.inf)
        m_ref[hh], l_ref[hh], acc_ref[hh] = _softmax_step(s, v2, m_ref[hh], l_ref[hh], acc_ref[hh])

    @pl.when(j == 0)
    def _():
        o_ref[...] = jnp.where(lane < ATT_DH, acc_ref[0] / l_ref[0], acc_ref[1] / l_ref[1])


def _moba_prompt(proj, bias):
    b, t, _ = proj.shape
    tq = min(ATT_TILE, t)
    assert tq == MOBA_BLOCK, "the prompt MoBA kernel visits one key block per tile"
    it, jt = _tri_tables(t // tq)
    grid_spec = pltpu.PrefetchScalarGridSpec(
        num_scalar_prefetch=2,
        grid=(b, ATT_HEADS // 2, int(it.shape[0])),
        in_specs=[
            pl.BlockSpec((None, tq, LANE), lambda i, p, s, it, jt: (i, it[s], OFF_MBQ // LANE + p)),
            pl.BlockSpec((None, tq, LANE), lambda i, p, s, it, jt: (i, jt[s], OFF_MBK // LANE + p)),
            pl.BlockSpec((None, tq, LANE), lambda i, p, s, it, jt: (i, jt[s], OFF_MBV // LANE + p)),
            pl.BlockSpec((None, 2, tq, LANE), lambda i, p, s, it, jt: (i, p, it[s], 0)),
        ],
        out_specs=pl.BlockSpec((None, tq, LANE), lambda i, p, s, it, jt: (i, it[s], p)),
        scratch_shapes=[pltpu.VMEM((2, tq, LANE), F32), pltpu.VMEM((2, tq, 1), F32), pltpu.VMEM((2, tq, 1), F32)],
    )
    return pl.pallas_call(
        _moba_prompt_kernel,
        grid_spec=grid_spec,
        out_shape=jax.ShapeDtypeStruct((b, t, ATT_HEADS * ATT_DH), F32),
        compiler_params=_params("parallel", "parallel", "arbitrary"),
    )(it, jt, proj, proj, proj, bias)


def _kmean_sample_kernel(pt_ref, l_ref, kp_ref, o_ref, *, pages_per_block):
    jj = pl.program_id(1)
    part = jnp.sum(kp_ref[...].reshape(PAGE_SIZE, ATT_HEADS, ATT_DH), axis=0) * (1.0 / MOBA_BLOCK)

    @pl.when(jj % pages_per_block == 0)
    def _():
        o_ref[...] = part

    @pl.when(jj % pages_per_block != 0)
    def _():
        o_ref[...] += part


def _kmean_sample(cache_k, page_table, layer):
    b, n_pages = page_table.shape
    ppb = MOBA_BLOCK // PAGE_SIZE
    page_rows = cache_k.shape[2]
    grid_spec = pltpu.PrefetchScalarGridSpec(
        num_scalar_prefetch=2,
        grid=(b, n_pages),
        in_specs=[pl.BlockSpec((None, None, page_rows, ATT_DH), lambda i, jj, pt, l: (l[0], pt[i, jj], 0, 0))],
        out_specs=pl.BlockSpec((None, None, ATT_HEADS, ATT_DH), lambda i, jj, pt, l: (i, jj // ppb, 0, 0)),
    )
    return pl.pallas_call(
        functools.partial(_kmean_sample_kernel, pages_per_block=ppb),
        grid_spec=grid_spec,
        out_shape=jax.ShapeDtypeStruct((b, n_pages // ppb, ATT_HEADS, ATT_DH), F32),
        compiler_params=_params("parallel", "arbitrary"),
    )(page_table, layer, cache_k)


def _select_sample_kernel(q_ref, km_ref, o_ref, *, own):
    nq = q_ref.shape[0] // ATT_HEADS
    q = q_ref[...]
    scores = jnp.concatenate(
        [lax.dot_general(q[nq * h:nq * (h + 1)], km_ref[h], NT_DIMS, precision=lax.Precision.HIGHEST,
                         preferred_element_type=F32) for h in range(ATT_HEADS)], axis=0)
    o_ref[...] = _select_bias(scores, jnp.full((scores.shape[0], 1), own, jnp.int32))


def _select_sample(q, kmean, own):
    b, rows, _ = q.shape
    return pl.pallas_call(
        functools.partial(_select_sample_kernel, own=own),
        grid=(b,),
        in_specs=[
            pl.BlockSpec((None, rows, ATT_DH), lambda i: (i, 0, 0)),
            pl.BlockSpec((None, ATT_HEADS, LANE, ATT_DH), lambda i: (i, 0, 0, 0)),
        ],
        out_specs=pl.BlockSpec((None, rows, LANE), lambda i: (i, 0, 0)),
        out_shape=jax.ShapeDtypeStruct((b, rows, LANE), F32),
        compiler_params=_params("parallel"),
    )(q, kmean)


def _moba_sample_kernel(pt_ref, l_ref, q_ref, kn_ref, vn_ref, bias_ref, kp_ref, vp_ref, o_ref,
                        acc_ref, m_ref, l_acc_ref, *, pages_per_block):
    jj = pl.program_id(1)
    n_pages = pl.num_programs(1)
    nq = q_ref.shape[0] // ATT_HEADS
    q = (q_ref[...] * (ATT_DH ** -0.5)).astype(BF16)
    rows = ATT_HEADS * nq

    def visit(get_k, get_v, bias, visible):
        s = jnp.concatenate(
            [_dot_nt(q[nq * h:nq * (h + 1)], get_k(h).astype(BF16)) for h in range(ATT_HEADS)], axis=0) + bias
        if visible is not None:
            s = jnp.where(visible, s, -jnp.inf)
        m_prev = m_ref[...]
        m_new = jnp.maximum(m_prev, jnp.max(s, axis=1, keepdims=True))
        p = jnp.exp(s - m_new)
        alpha = jnp.exp(m_prev - m_new)
        pb = p.astype(BF16)
        pv = jnp.concatenate(
            [_dot(pb[nq * h:nq * (h + 1)], get_v(h).astype(BF16)) for h in range(ATT_HEADS)], axis=0)
        m_ref[...] = m_new
        l_acc_ref[...] = alpha * l_acc_ref[...] + jnp.sum(p, axis=1, keepdims=True)
        acc_ref[...] = alpha * acc_ref[...] + pv

    @pl.when(jj == 0)
    def _():
        acc_ref[...] = jnp.zeros(acc_ref.shape, F32)
        m_ref[...] = jnp.full(m_ref.shape, -jnp.inf, F32)
        l_acc_ref[...] = jnp.zeros(l_acc_ref.shape, F32)
        qrow = _iota((rows, PAGE_SIZE), 0) & (nq - 1)
        visit(lambda h: kn_ref[h], lambda h: vn_ref[h], 0.0, _iota((rows, PAGE_SIZE), 1) <= qrow)

    blk = (n_pages - 1 - jj) // pages_per_block
    bias_all = bias_ref[...]
    bias = jnp.sum(jnp.where(_iota(bias_all.shape, 1) == blk, bias_all, 0.0), axis=1, keepdims=True)
    visit(lambda h: _head_rows(kp_ref, h), lambda h: _head_rows(vp_ref, h), bias, None)

    @pl.when(jj == n_pages - 1)
    def _():
        o_ref[...] = acc_ref[...] / l_acc_ref[...]


def _moba_sample(q, k_new, v_new, bias, cache_k, cache_v, page_table, layer):
    b, rows, _ = q.shape
    n_pages = page_table.shape[1]
    page_rows = cache_k.shape[2]
    page = lambda i, jj, pt, l: (l[0], pt[i, n_pages - 1 - jj], 0, 0)
    per_b3 = lambda i, jj, pt, l: (i, 0, 0)
    per_b4 = lambda i, jj, pt, l: (i, 0, 0, 0)
    grid_spec = pltpu.PrefetchScalarGridSpec(
        num_scalar_prefetch=2,
        grid=(b, n_pages),
        in_specs=[
            pl.BlockSpec((None, rows, ATT_DH), per_b3),
            pl.BlockSpec((None, ATT_HEADS, PAGE_SIZE, ATT_DH), per_b4),
            pl.BlockSpec((None, ATT_HEADS, PAGE_SIZE, ATT_DH), per_b4),
            pl.BlockSpec((None, rows, LANE), per_b3),
            pl.BlockSpec((None, None, page_rows, ATT_DH), page),
            pl.BlockSpec((None, None, page_rows, ATT_DH), page),
        ],
        out_specs=pl.BlockSpec((None, rows, ATT_DH), per_b3),
        scratch_shapes=[pltpu.VMEM((rows, ATT_DH), F32), pltpu.VMEM((rows, 1), F32), pltpu.VMEM((rows, 1), F32)],
    )
    return pl.pallas_call(
        functools.partial(_moba_sample_kernel, pages_per_block=MOBA_BLOCK // PAGE_SIZE),
        grid_spec=grid_spec,
        out_shape=jax.ShapeDtypeStruct((b, rows, ATT_DH), F32),
        compiler_params=_params("parallel", "arbitrary"),
    )(page_table, layer, q, k_new, v_new, bias, cache_k, cache_v)


def _merge_kernel(o0_ref, o1_ref, o2_ref, o3_ref, g0_ref, g1_ref, g2_ref, g3_ref, x_ref, wb_ref, wo_ref, nw_ref,
                  x1_ref, hf_ref):
    mixed = None
    for o_ref, g_ref, i in ((o0_ref, g0_ref, 0), (o1_ref, g1_ref, 1), (o2_ref, g2_ref, 2), (o3_ref, g3_ref, 3)):
        term = _sigmoid(g_ref[...]) * _dot(o_ref[...].astype(BF16), wb_ref[i])
        mixed = term if mixed is None else mixed + term
    x1 = x_ref[...] + _dot(mixed.astype(BF16), wo_ref[...])
    x1_ref[...] = x1
    ms = jnp.mean(x1 * x1, axis=-1, keepdims=True)
    hf_ref[...] = (x1 * lax.rsqrt(ms + EPS) * nw_ref[...]).astype(BF16)


def _merge(branches, proj, x, lp):
    n, d = x.shape
    tm = _tile(n, 256)
    row = lambda i: (i, 0)
    gate = lambda g: (lambda i: (i, OFF_GATE // d + g))
    return pl.pallas_call(
        _merge_kernel,
        grid=(n // tm,),
        in_specs=[pl.BlockSpec((tm, BRANCH_WIDTH), row)] * 4
        + [pl.BlockSpec((tm, d), gate(g)) for g in range(4)]
        + [
            pl.BlockSpec((tm, d), row),
            pl.BlockSpec((4, BRANCH_WIDTH, d), lambda i: (0, 0, 0)),
            pl.BlockSpec((d, d), lambda i: (0, 0)),
            pl.BlockSpec((1, d), lambda i: (0, 0)),
        ],
        out_specs=[pl.BlockSpec((tm, d), row), pl.BlockSpec((tm, d), row)],
        out_shape=[jax.ShapeDtypeStruct((n, d), F32), jax.ShapeDtypeStruct((n, d), BF16)],
        compiler_params=_params("parallel"),
    )(*branches, proj, proj, proj, proj, x, lp["w_branch"], lp["w_out"], lp["norm_ffn"])


def _ffn_kernel(u_ref, halo_ref, halo0_ref, x1_ref, cw_ref, cb_ref, wd_ref, o_ref, ext_ref, *, tiles_per_seq):
    i = pl.program_id(0)
    tm = u_ref.shape[0]
    seq_start = (i % tiles_per_seq) == 0

    @pl.when(seq_start)
    def _():
        ext_ref[0:SUBLANE, :] = halo0_ref[...]

    @pl.when(jnp.logical_not(seq_start))
    def _():
        ext_ref[0:SUBLANE, :] = halo_ref[...]

    u = u_ref[...]
    ext_ref[SUBLANE:SUBLANE + tm, :] = u
    cw = cw_ref[...]
    conv = u * cw[FFN_CONV - 1:FFN_CONV] + cb_ref[...]
    for t in range(FFN_CONV - 1):
        off = SUBLANE - (FFN_CONV - 1) + t
        conv = conv + ext_ref[off:off + tm, :] * cw[t:t + 1]
    act = _silu(conv[:, :D_FF]) * conv[:, D_FF:]
    o_ref[...] = x1_ref[...] + _dot(act.astype(BF16), wd_ref[...])


def _ffn(u, halo0, x1, seq_len, lp):
    n, w = u.shape
    d = x1.shape[1]
    tm = _tile(seq_len, 256)
    tiles_per_seq = seq_len // tm
    hb = tm // SUBLANE
    return pl.pallas_call(
        functools.partial(_ffn_kernel, tiles_per_seq=tiles_per_seq),
        grid=(n // tm,),
        in_specs=[
            pl.BlockSpec((tm, w), lambda i: (i, 0)),
            pl.BlockSpec((SUBLANE, w), lambda i: (jnp.maximum(i * hb - 1, 0), 0)),
            pl.BlockSpec((None, SUBLANE, w), lambda i: (i // tiles_per_seq, 0, 0)),
            pl.BlockSpec((tm, d), lambda i: (i, 0)),
            pl.BlockSpec((FFN_CONV, w), lambda i: (0, 0)),
            pl.BlockSpec((1, w), lambda i: (0, 0)),
            pl.BlockSpec((D_FF, d), lambda i: (0, 0)),
        ],
        out_specs=pl.BlockSpec((tm, d), lambda i: (i, 0)),
        out_shape=jax.ShapeDtypeStruct((n, d), F32),
        scratch_shapes=[pltpu.VMEM((tm + SUBLANE, w), F32)],
        compiler_params=_params("parallel"),
    )(u, u, halo0, x1, lp["ffn_conv_w"], lp["ffn_conv_b"], lp["w_down"])


def _rope_tables(pos0, t):
    half = RET_DK // 2
    inv = ROPE_BASE ** (-jnp.arange(half, dtype=F32) / half)
    ang = (pos0 + jnp.arange(t, dtype=jnp.int32)).astype(F32)[:, None] * inv[None, :]
    cos, sin = jnp.cos(ang), jnp.sin(ang)
    cos_t = jnp.tile(jnp.concatenate([cos, cos], axis=1), (1, RET_HEADS))
    sin_t = jnp.tile(jnp.concatenate([-sin, sin], axis=1), (1, RET_HEADS))
    return cos_t, sin_t


def _pad_state_rows(s):
    return jnp.pad(s, ((0, 0), (SUBLANE - s.shape[1], 0), (0, 0)))


def _ssm_state_in(h0):
    b = h0.shape[0]
    h2 = h0.reshape(b, SSM_INNER, SSM_STATE)
    g0 = (jnp.arange(SSM_INNER) < SSM_INNER // SSM_GROUPS)[None, :, None]
    return jnp.concatenate([jnp.where(g0, h2, 0.0), jnp.where(g0, 0.0, h2)], axis=-1)


def _ssm_state_out(h2):
    b = h2.shape[0]
    g0 = (jnp.arange(SSM_INNER) < SSM_INNER // SSM_GROUPS)[None, :, None]
    return jnp.where(g0, h2[..., :SSM_STATE], h2[..., SSM_STATE:]).reshape(b, SSM_HEADS, SSM_HEADDIM, SSM_STATE)


def _prep_layer_weights(w):
    depth = w["w_in"].shape[0]
    w_in = w["w_in"]
    e_xbc = OFF_Z + SSM_INNER + SSM_CONV_DIM
    w_in = jnp.concatenate([
        w_in[..., :OFF_Z + SSM_INNER],
        w_in[..., e_xbc:e_xbc + SSM_HEADS],
        jnp.zeros(w_in.shape[:2] + (DT_PAD - SSM_HEADS,), w_in.dtype),
        w_in[..., OFF_Z + SSM_INNER:e_xbc],
        w_in[..., e_xbc + SSM_HEADS:],
    ], axis=-1)
    assert w_in.shape[-1] == P_TOTAL
    pad_heads = lambda a: jnp.pad(a, ((0, 0), (0, LANE - SSM_HEADS)))[:, None, :]
    return {
        "norm_mix": w["norm_mix"],
        "w_in": w_in.astype(BF16),
        "ssm_conv_w": w["ssm_conv_w"],
        "ssm_conv_b": w["ssm_conv_b"][:, None, :],
        "ssm_dt_bias": pad_heads(w["ssm_dt_bias"]),
        "ssm_a_log": pad_heads(w["ssm_a_log"]),
        "ssm_d": jnp.repeat(w["ssm_d"], SSM_HEADDIM, axis=1)[:, None, :],
        "ssm_norm": w["ssm_norm"][:, None, :],
        "w_branch": w["w_branch"].astype(BF16),
        "w_out": w["w_out"].astype(BF16),
        "norm_ffn": w["norm_ffn"][:, None, :],
        "w_up": w["w_up"].astype(BF16),
        "ffn_conv_w": w["ffn_conv_w"],
        "ffn_conv_b": w["ffn_conv_b"][:, None, :],
        "w_down": w["w_down"].astype(BF16),
        "layer": jnp.arange(depth, dtype=jnp.int32)[:, None],
    }


def _heads_major(a, b, t):
    return a.reshape(b, t, ATT_HEADS, ATT_DH).transpose(0, 2, 1, 3)


def _layer(x, lp, past, pos0, caches, page_table):
    b, t, d = x.shape
    n = b * t
    r0, h0, conv0, ffn0 = past
    x2d = x.reshape(n, d)
    hn = _rmsnorm(x2d, lp["norm_mix"], BF16)
    proj = _matmul(hn, lp["w_in"])
    proj3 = proj.reshape(b, t, P_TOTAL)
    w_att = ATT_HEADS * ATT_DH

    cos, sin = _rope_tables(pos0, t)
    o_ret, r_new = _retention(proj3, cos, sin, r0)
    o_ssm, conv_new, h_new = _ssd(proj3, _pad_state_rows(conv0), _ssm_state_in(h0), lp)

    sbk, sbv = proj[:, OFF_SBK:OFF_SBK + w_att], proj[:, OFF_SBV:OFF_SBV + w_att]
    mbk, mbv = proj[:, OFF_MBK:OFF_MBK + w_att], proj[:, OFF_MBV:OFF_MBV + w_att]
    if caches is None:
        o_sb = _sb_prompt(proj3)
        kmean = _kmean_prompt(proj3)[:, :, 0, :]
        kmean = jnp.pad(kmean, ((0, 0), (0, LANE - kmean.shape[1]), (0, 0)))
        o_mb = _moba_prompt(proj3, _select_prompt(proj3, kmean))
    else:
        csk, csv, cmk, cmv = caches
        layer = lp["layer"]
        rows = ATT_HEADS * t
        pad_new = lambda a: jnp.pad(_heads_major(a, b, t), ((0, 0), (0, 0), (0, PAGE_SIZE - t), (0, 0)))
        to_rows = lambda a: _heads_major(a, b, t).reshape(b, rows, ATT_DH)
        from_rows = lambda o: o.reshape(b, ATT_HEADS, t, ATT_DH).transpose(0, 2, 1, 3).reshape(b, t, w_att)
        sq = to_rows(proj[:, OFF_SBQ:OFF_SBQ + w_att])
        o_sb = from_rows(_sb_sample(sq, pad_new(sbk), pad_new(sbv), csk, csv, page_table, layer))
        mq = to_rows(proj[:, OFF_MBQ:OFF_MBQ + w_att])
        kmean = _kmean_sample(cmk, page_table, layer).transpose(0, 2, 1, 3)
        n_blocks = kmean.shape[2]
        kmean = jnp.pad(kmean, ((0, 0), (0, 0), (0, LANE - n_blocks), (0, 0)))
        bias = _select_sample(mq, kmean, n_blocks)
        o_mb = from_rows(_moba_sample(mq, pad_new(mbk), pad_new(mbv), bias, cmk, cmv, page_table, layer))

    branches = [o.reshape(n, BRANCH_WIDTH) for o in (o_ret, o_sb, o_mb, o_ssm)]
    x1, hf = _merge(branches, proj, x2d, lp)
    u = _matmul(hf, lp["w_up"])
    x2 = _ffn(u, _pad_state_rows(ffn0), x1, t, lp)
    heads = lambda a: a.reshape(b, t, ATT_HEADS, ATT_DH)
    state = (heads(sbk), heads(sbv), heads(mbk), heads(mbv), r_new, _ssm_state_out(h_new),
             conv_new[:, SUBLANE - (SSM_CONV - 1):], u.reshape(b, t, -1)[:, t - (FFN_CONV - 1):])
    return x2.reshape(b, t, d), state


def _trunk(x, pos0, past, lw, norm_final, caches, page_table):
    def body(carry, per_layer):
        lp, pst = per_layer
        y, st = _layer(carry, lp, pst, pos0, caches, page_table)
        return y, st

    y, states = lax.scan(body, x, (lw, past))
    b, t, d = y.shape
    out = _rmsnorm(y.reshape(b * t, d), norm_final, F32).reshape(b, t, d)
    return out, states


def kernel(x_prompt, x_sample, cache_sb_k, cache_sb_v, cache_moba_k, cache_moba_v, page_table, state_ret, state_ssm,
           state_ssm_conv, state_ffn_conv, norm_mix, w_in, ssm_conv_w, ssm_conv_b, ssm_dt_bias, ssm_a_log, ssm_d,
           ssm_norm, w_branch, w_out, norm_ffn, w_up, ffn_conv_w, ffn_conv_b, w_down, norm_final):
    lw = _prep_layer_weights({
        "norm_mix": norm_mix, "w_in": w_in, "ssm_conv_w": ssm_conv_w, "ssm_conv_b": ssm_conv_b,
        "ssm_dt_bias": ssm_dt_bias, "ssm_a_log": ssm_a_log, "ssm_d": ssm_d, "ssm_norm": ssm_norm,
        "w_branch": w_branch, "w_out": w_out, "norm_ffn": norm_ffn, "w_up": w_up, "ffn_conv_w": ffn_conv_w,
        "ffn_conv_b": ffn_conv_b, "w_down": w_down})
    depth = w_in.shape[0]
    bp = x_prompt.shape[0]
    zeros = lambda *s: jnp.zeros((depth, bp) + s, F32)
    past_p = (zeros(RET_HEADS, RET_DK, RET_DV), zeros(SSM_HEADS, SSM_HEADDIM, SSM_STATE),
              zeros(SSM_CONV - 1, SSM_CONV_DIM), zeros(FFN_CONV - 1, 2 * D_FF))
    y_p, st_p = _trunk(x_prompt, 0, past_p, lw, norm_final, None, None)

    as_rows = lambda c: c.reshape(c.shape[0], c.shape[1], c.shape[2] * c.shape[3], c.shape[4])
    caches = tuple(as_rows(c) for c in (cache_sb_k, cache_sb_v, cache_moba_k, cache_moba_v))
    past_s = (state_ret, state_ssm, state_ssm_conv, state_ffn_conv)
    pos0 = page_table.shape[1] * PAGE_SIZE
    y_s, st_s = _trunk(x_sample, pos0, past_s, lw, norm_final, caches, page_table)
    return (y_p, y_s) + tuple(st_p) + tuple(st_s)
```

```python
import functools
import math

import jax
import jax.numpy as jnp
from jax import lax
from jax.experimental import pallas as pl
from jax.experimental.pallas import tpu as pltpu

F32 = jnp.float32
BF16 = jnp.bfloat16

EPS = 1e-6
D_MODEL = 1024
BRANCH_WIDTH = 512
RET_HEADS, RET_DK, RET_DV = 4, 64, 128
ROPE_BASE = 10000.0
ATT_HEADS, ATT_DH = 8, 64
MOBA_BLOCK, MOBA_TOPK = 256, 3
SSM_HEADS, SSM_HEADDIM, SSM_STATE, SSM_GROUPS, SSM_CONV = 8, 64, 64, 2, 4
SSM_INNER = SSM_HEADS * SSM_HEADDIM
SSM_CONV_DIM = SSM_INNER + 2 * SSM_GROUPS * SSM_STATE
D_FF = 2816
FFN_CONV = 3
PAGE_SIZE = 128
CHUNK = 128
ATT_TILE = 256
LOG_GAMMA = tuple(math.log1p(-2.0 ** (-5.0 - h)) for h in range(RET_HEADS))

OFF_RQ, OFF_RK, OFF_RV, OFF_RG = 0, 256, 512, 1024
OFF_SBQ, OFF_SBK, OFF_SBV = 1536, 2048, 2560
OFF_MBQ, OFF_MBK, OFF_MBV = 3072, 3584, 4096
OFF_Z, OFF_DT, OFF_XBC, OFF_GATE = 4608, 5120, 5376, 6144
DT_PAD = 256
P_TOTAL = OFF_GATE + 4 * D_MODEL
NEG_BIAS = -1e30
SB_DONE = 104.0
SB_FIRST_PAGES = 4

LANE = 128
SUBLANE = 8
VMEM_LIMIT = 56 * 1024 * 1024

NT_DIMS = (((1,), (1,)), ((), ()))
TN_DIMS = (((0,), (0,)), ((), ()))


def _dot(a, b):
    return jnp.dot(a, b, preferred_element_type=F32)


def _dot_nt(a, b):
    return lax.dot_general(a, b, NT_DIMS, preferred_element_type=F32)


def _dot_tn(a, b):
    return lax.dot_general(a, b, TN_DIMS, preferred_element_type=F32)


def _split_bf16(x, n):
    terms, r = [], x
    for _ in range(n):
        t = r.astype(BF16)
        terms.append(t)
        r = r - t.astype(F32)
    return terms


def _dot_split_rhs(a01, x, n=3):
    out = None
    for t in _split_bf16(x, n):
        d = _dot(a01, t)
        out = d if out is None else out + d
    return out


def _dot_split_lhs(x, a01, n=3):
    out = None
    for t in _split_bf16(x, n):
        d = _dot(t, a01)
        out = d if out is None else out + d
    return out


def _iota(shape, dim):
    return lax.broadcasted_iota(jnp.int32, shape, dim)


def _sigmoid(x):
    return 1.0 / (1.0 + jnp.exp(-x))


def _silu(x):
    return x * _sigmoid(x)


def _log_sigmoid(x):
    return jnp.minimum(x, 0.0) - jnp.log1p(jnp.exp(-jnp.abs(x)))


def _softplus(x):
    return jnp.maximum(x, 0.0) + jnp.log1p(jnp.exp(-jnp.abs(x)))


def _pad_rows(x, rows):
    if x.shape[0] == rows:
        return x
    return jnp.concatenate([x, jnp.zeros((rows - x.shape[0],) + x.shape[1:], x.dtype)], axis=0)


def _tile(n, cap, mult=SUBLANE):
    if n <= cap:
        return n
    for t in range(cap - cap % mult, 0, -mult):
        if n % t == 0:
            return t
    raise ValueError(f"no tile for {n}")


def _params(*sem):
    return pltpu.CompilerParams(dimension_semantics=sem, vmem_limit_bytes=VMEM_LIMIT)


def _rmsnorm_kernel(x_ref, w_ref, o_ref):
    x = x_ref[...]
    ms = jnp.mean(x * x, axis=-1, keepdims=True)
    o_ref[...] = (x * lax.rsqrt(ms + EPS) * w_ref[...]).astype(o_ref.dtype)


def _rmsnorm(x, w, out_dtype):
    n, d = x.shape
    tm = _tile(n, 512)
    return pl.pallas_call(
        _rmsnorm_kernel,
        grid=(n // tm,),
        in_specs=[pl.BlockSpec((tm, d), lambda i: (i, 0)), pl.BlockSpec((1, d), lambda i: (0, 0))],
        out_specs=pl.BlockSpec((tm, d), lambda i: (i, 0)),
        out_shape=jax.ShapeDtypeStruct((n, d), out_dtype),
        compiler_params=_params("parallel"),
    )(x, w.reshape(1, d))


def _matmul_kernel(x_ref, w_ref, o_ref):
    o_ref[...] = _dot(x_ref[...], w_ref[...])


def _matmul(x, w):
    n, k = x.shape
    m = w.shape[1]
    tm = _tile(n, 1024)
    tn = _tile(m, 1536, LANE)
    return pl.pallas_call(
        _matmul_kernel,
        grid=(m // tn, n // tm),
        in_specs=[pl.BlockSpec((tm, k), lambda c, r: (r, 0)), pl.BlockSpec((k, tn), lambda c, r: (0, c))],
        out_specs=pl.BlockSpec((tm, tn), lambda c, r: (r, c)),
        out_shape=jax.ShapeDtypeStruct((n, m), F32),
        compiler_params=_params("parallel", "parallel"),
    )(x, w)


def _retention_kernel(q_ref, k_ref, v_ref, g_ref, cos_ref, sin_ref, r0_ref, o_ref, rn_ref, r_ref, *, c, tv):
    n = pl.program_id(1)
    hk, hv = RET_HEADS * RET_DK, RET_HEADS * RET_DV

    @pl.when(n == 0)
    def _():
        r_ref[...] = jnp.zeros((hk, hv), F32)
        for h in range(RET_HEADS):
            r_ref[RET_DK * h:RET_DK * (h + 1), RET_DV * h:RET_DV * (h + 1)] = r0_ref[h]

    rows = q_ref.shape[0]
    q, k, v = _pad_rows(q_ref[...], c), _pad_rows(k_ref[...], c), _pad_rows(v_ref[...], c)
    cos, sin = _pad_rows(cos_ref[...], c), _pad_rows(sin_ref[...], c)
    lane = _iota((c, hk), 1)
    head_of_lane = lane >> 6
    first_half = (_iota((c, LANE), 1) & 63) < 32

    def rope(x):
        parts = []
        for s in range(hk // LANE):
            xs = x[:, LANE * s:LANE * (s + 1)]
            parts.append(jnp.where(first_half, pltpu.roll(xs, LANE - 32, 1), pltpu.roll(xs, 32, 1)))
        return x * cos + jnp.concatenate(parts, axis=1) * sin

    qr = rope(q) * (RET_DK ** -0.5)
    kr = rope(k)
    lg = jnp.full((c, hk), LOG_GAMMA[RET_HEADS - 1], F32)
    for h in range(RET_HEADS - 2, -1, -1):
        lg = jnp.where(head_of_lane == h, LOG_GAMMA[h], lg)
    ri = _iota((c, hk), 0).astype(F32)
    q_from_start = jnp.exp(lg * (ri + 1.0))
    k_to_end = jnp.exp(lg * (float(tv - 1) - ri))
    kb, vb = kr.astype(BF16), v.astype(BF16)
    di = _iota((c, c), 0) - _iota((c, c), 1)
    dif = jnp.maximum(di, 0).astype(F32)
    inner = []
    for h in range(RET_HEADS):
        decay = jnp.where(di >= 0, jnp.exp(LOG_GAMMA[h] * dif), 0.0)
        qm = jnp.where(head_of_lane == h, qr, 0.0).astype(BF16)
        s = _dot_nt(qm, kb)
        inner.append(_dot((s * decay).astype(BF16), vb[:, RET_DV * h:RET_DV * (h + 1)]))
    r = r_ref[...]
    cross = _dot((qr * q_from_start).astype(BF16), r.astype(BF16))
    ret = jnp.concatenate(inner, axis=1) + cross
    kv = _dot_tn((kr * k_to_end).astype(BF16), vb)
    row_head = _iota((hk, hv), 0) >> 6
    col_head = _iota((hk, hv), 1) >> 7
    g_chunk = jnp.full((hk, hv), math.exp(LOG_GAMMA[RET_HEADS - 1] * tv), F32)
    for h in range(RET_HEADS - 2, -1, -1):
        g_chunk = jnp.where(row_head == h, math.exp(LOG_GAMMA[h] * tv), g_chunk)
    r_new = g_chunk * r + jnp.where(row_head == col_head, kv, 0.0)
    r_ref[...] = r_new
    g = _pad_rows(g_ref[...], c)
    outs = []
    for h in range(RET_HEADS):
        x = ret[:, RET_DV * h:RET_DV * (h + 1)]
        y = x * lax.rsqrt(jnp.mean(x * x, axis=-1, keepdims=True) + EPS)
        outs.append(y * _silu(g[:, RET_DV * h:RET_DV * (h + 1)]))
    o_ref[...] = jnp.concatenate(outs, axis=1)[:rows]

    @pl.when(n == pl.num_programs(1) - 1)
    def _():
        for h in range(RET_HEADS):
            rn_ref[h] = r_new[RET_DK * h:RET_DK * (h + 1), RET_DV * h:RET_DV * (h + 1)]


def _retention(proj, cos, sin, r0):
    b, t, _ = proj.shape
    cb = min(t, CHUNK)
    hk, hv = RET_HEADS * RET_DK, RET_HEADS * RET_DV
    kern = functools.partial(_retention_kernel, c=CHUNK, tv=cb)
    return pl.pallas_call(
        kern,
        grid=(b, t // cb),
        in_specs=[
            pl.BlockSpec((None, cb, hk), lambda i, n: (i, n, OFF_RQ // hk)),
            pl.BlockSpec((None, cb, hk), lambda i, n: (i, n, OFF_RK // hk)),
            pl.BlockSpec((None, cb, hv), lambda i, n: (i, n, OFF_RV // hv)),
            pl.BlockSpec((None, cb, hv), lambda i, n: (i, n, OFF_RG // hv)),
            pl.BlockSpec((cb, hk), lambda i, n: (n, 0)),
            pl.BlockSpec((cb, hk), lambda i, n: (n, 0)),
            pl.BlockSpec((None, RET_HEADS, RET_DK, RET_DV), lambda i, n: (i, 0, 0, 0)),
        ],
        out_specs=[
            pl.BlockSpec((None, cb, hv), lambda i, n: (i, n, 0)),
            pl.BlockSpec((None, RET_HEADS, RET_DK, RET_DV), lambda i, n: (i, 0, 0, 0)),
        ],
        out_shape=[jax.ShapeDtypeStruct((b, t, hv), F32), jax.ShapeDtypeStruct(r0.shape, F32)],
        scratch_shapes=[pltpu.VMEM((hk, hv), F32)],
        compiler_params=_params("parallel", "arbitrary"),
    )(proj, proj, proj, proj, cos, sin, r0)


def _ssd_kernel(z_ref, dt_ref, xbc_ref, conv0_ref, h0_ref, cw_ref, cb_ref, dtb_ref, alog_ref, dx_ref, nw_ref,
                o_ref, convn_ref, hn_ref, xp_ref, h_ref, *, c, tv):
    n = pl.program_id(1)
    rows = xbc_ref.shape[0]
    half = SSM_INNER // SSM_GROUPS

    @pl.when(n == 0)
    def _():
        xp_ref[0:SUBLANE, :] = conv0_ref[...]
        h_ref[...] = h0_ref[...]

    xraw = _pad_rows(xbc_ref[...], c)
    xp_ref[SUBLANE:SUBLANE + c, :] = xraw
    cw = cw_ref[...]
    acc = xraw * cw[SSM_CONV - 1:SSM_CONV] + cb_ref[...]
    for i in range(SSM_CONV - 1):
        off = SUBLANE - (SSM_CONV - 1) + i
        acc = acc + xp_ref[off:off + c, :] * cw[i:i + 1]
    tail = xp_ref[tv:tv + SUBLANE, :]
    xp_ref[0:SUBLANE, :] = tail
    xbc = _silu(acc)
    xs = xbc[:, :SSM_INNER]
    bm = xbc[:, SSM_INNER:SSM_INNER + LANE]
    cm = xbc[:, SSM_INNER + LANE:SSM_INNER + 2 * LANE]

    dt = _softplus(_pad_rows(dt_ref[...], c) + dtb_ref[...])
    if tv < c:
        dt = jnp.where(_iota((c, LANE), 0) < tv, dt, 0.0)
    da = dt * (-jnp.exp(alog_ref[...]))
    tril = _iota((c, c), 0) >= _iota((c, c), 1)
    cum = _dot_split_rhs(tril.astype(BF16), da)
    cum_t = cum.T
    expand = ((_iota((LANE, SSM_INNER), 1) >> 6) == _iota((LANE, SSM_INNER), 0)).astype(BF16)
    cumx = _dot_split_lhs(cum, expand)
    dtx = _dot_split_lhs(dt, expand)
    to_end = jnp.exp(cumx[tv - 1:tv, :] - cumx) * dtx
    x_to_end = (xs * to_end).astype(BF16)
    x_dt = (xs * dtx).astype(BF16)
    bb = bm.astype(BF16)
    lane = _iota((c, LANE), 1)
    cg = [jnp.where((lane >> 6) == g, cm, 0.0).astype(BF16) for g in range(SSM_GROUPS)]
    scores = [_dot_nt(cg[g], bb) for g in range(SSM_GROUPS)]
    y_pairs = []
    for p in range(SSM_HEADS // 2):
        y_head = []
        for hh in range(2):
            h = 2 * p + hh
            seg = cum[:, h:h + 1] - cum_t[h:h + 1, :]
            m = scores[h // (SSM_HEADS // SSM_GROUPS)] * jnp.exp(jnp.where(tril, seg, -jnp.inf))
            y_head.append(_dot(m.astype(BF16), x_dt[:, LANE * p:LANE * (p + 1)]))
        y_pairs.append(jnp.where(lane < SSM_HEADDIM, y_head[0], y_head[1]))
    hs = h_ref[...]
    hsb = hs.astype(BF16)
    y_off = jnp.concatenate([_dot_nt(cg[g], hsb[half * g:half * (g + 1)]) for g in range(SSM_GROUPS)], axis=1)
    y = jnp.concatenate(y_pairs, axis=1) + y_off * jnp.exp(cumx) + dx_ref[...] * xs
    tz = y * _silu(_pad_rows(z_ref[...], c))
    out = tz * lax.rsqrt(jnp.mean(tz * tz, axis=-1, keepdims=True) + EPS) * nw_ref[...]
    o_ref[...] = out[:rows]

    expand_t = ((_iota((SSM_INNER, LANE), 0) >> 6) == _iota((SSM_INNER, LANE), 1)).astype(BF16)
    cum_last = jnp.broadcast_to(cum_t[:, tv - 1:tv], (LANE, LANE))
    chunk_decay = jnp.exp(_dot_split_rhs(expand_t, cum_last))
    contrib = jnp.concatenate([_dot_tn(x_to_end[:, half * g:half * (g + 1)], bb) for g in range(SSM_GROUPS)], axis=0)
    h_new = chunk_decay * hs + contrib
    h_ref[...] = h_new

    @pl.when(n == pl.num_programs(1) - 1)
    def _():
        hn_ref[...] = h_new
        convn_ref[...] = tail


def _ssd(proj, conv0, h0, lp):
    b, t, _ = proj.shape
    cb = min(t, CHUNK)
    kern = functools.partial(_ssd_kernel, c=CHUNK, tv=cb)
    const = lambda i, n: (0, 0)
    return pl.pallas_call(
        kern,
        grid=(b, t // cb),
        in_specs=[
            pl.BlockSpec((None, cb, SSM_INNER), lambda i, n: (i, n, OFF_Z // SSM_INNER)),
            pl.BlockSpec((None, cb, LANE), lambda i, n: (i, n, OFF_DT // LANE)),
            pl.BlockSpec((None, cb, SSM_CONV_DIM), lambda i, n: (i, n, OFF_XBC // SSM_CONV_DIM)),
            pl.BlockSpec((None, SUBLANE, SSM_CONV_DIM), lambda i, n: (i, 0, 0)),
            pl.BlockSpec((None, SSM_INNER, LANE), lambda i, n: (i, 0, 0)),
            pl.BlockSpec((SSM_CONV, SSM_CONV_DIM), const),
            pl.BlockSpec((1, SSM_CONV_DIM), const),
            pl.BlockSpec((1, LANE), const),
            pl.BlockSpec((1, LANE), const),
            pl.BlockSpec((1, SSM_INNER), const),
            pl.BlockSpec((1, SSM_INNER), const),
        ],
        out_specs=[
            pl.BlockSpec((None, cb, SSM_INNER), lambda i, n: (i, n, 0)),
            pl.BlockSpec((None, SUBLANE, SSM_CONV_DIM), lambda i, n: (i, 0, 0)),
            pl.BlockSpec((None, SSM_INNER, LANE), lambda i, n: (i, 0, 0)),
        ],
        out_shape=[
            jax.ShapeDtypeStruct((b, t, SSM_INNER), F32),
            jax.ShapeDtypeStruct((b, SUBLANE, SSM_CONV_DIM), F32),
            jax.ShapeDtypeStruct((b, SSM_INNER, LANE), F32),
        ],
        scratch_shapes=[pltpu.VMEM((CHUNK + SUBLANE, SSM_CONV_DIM), F32), pltpu.VMEM((SSM_INNER, LANE), F32)],
        compiler_params=_params("parallel", "arbitrary"),
    )(proj, proj, proj, conv0, h0, lp["ssm_conv_w"], lp["ssm_conv_b"], lp["ssm_dt_bias"], lp["ssm_a_log"],
      lp["ssm_d"], lp["ssm_norm"])


def _sb_tile(z, earlier, carry, upper):
    ls = _log_sigmoid(z)
    lk = ls - z
    if earlier is not None:
        lk = jnp.where(earlier, lk, 0.0)
    within = _dot_split_lhs(lk, upper, 2)
    a = jnp.exp(ls + within + carry)
    if earlier is not None:
        a = jnp.where(earlier, a, 0.0)
    return a, carry + jnp.sum(lk, axis=1, keepdims=True)


def _upper_ones(tk):
    return (_iota((tk, tk), 0) > _iota((tk, tk), 1)).astype(BF16)


def _all_done(carry):
    c = carry if not isinstance(carry, (list, tuple)) else functools.reduce(jnp.maximum, carry)
    return jnp.max(c, axis=0, keepdims=True)[0, 0] <= -SB_DONE


def _sb_prompt_kernel(q_ref, k_ref, v_ref, o_ref, acc_ref, carry_ref):
    i = pl.program_id(2)
    tq = q_ref.shape[0]
    tk = tq
    acc_ref[...] = jnp.zeros(acc_ref.shape, F32)
    carry_ref[...] = jnp.zeros(carry_ref.shape, F32)
    q2 = q_ref[...] * (ATT_DH ** -0.5)
    lane = _iota((tq, LANE), 1)
    qm = [jnp.where((lane >= ATT_DH) == (hh == 1), q2, 0.0).astype(BF16) for hh in range(2)]
    upper = _upper_ones(tk)

    def tile(j, earlier):
        off = pl.multiple_of(j * tk, tk)
        k2 = k_ref[pl.ds(off, tk), :].astype(BF16)
        v2 = v_ref[pl.ds(off, tk), :].astype(BF16)
        for hh in range(2):
            a, carry = _sb_tile(_dot_nt(qm[hh], k2), earlier, carry_ref[hh], upper)
            acc_ref[hh] += _dot(a.astype(BF16), v2)
            carry_ref[hh] = carry
        return _all_done([carry_ref[0], carry_ref[1]])

    done = tile(i, _iota((tq, tk), 1) < _iota((tq, tk), 0))

    def body(state):
        j, _ = state
        return j - 1, tile(j, None).astype(jnp.int32)

    lax.while_loop(lambda st: (st[0] >= 0) & (st[1] == 0), body, (i - 1, done.astype(jnp.int32)))
    o_ref[...] = jnp.where(lane < ATT_DH, acc_ref[0], acc_ref[1])


def _sb_prompt(proj):
    b, t, _ = proj.shape
    tq = min(ATT_TILE, t)
    return pl.pallas_call(
        _sb_prompt_kernel,
        grid=(b, ATT_HEADS // 2, t // tq),
        in_specs=[
            pl.BlockSpec((None, tq, LANE), lambda i, p, n: (i, n, OFF_SBQ // LANE + p)),
            pl.BlockSpec((None, t, LANE), lambda i, p, n: (i, 0, OFF_SBK // LANE + p)),
            pl.BlockSpec((None, t, LANE), lambda i, p, n: (i, 0, OFF_SBV // LANE + p)),
        ],
        out_specs=pl.BlockSpec((None, tq, LANE), lambda i, p, n: (i, n, p)),
        out_shape=jax.ShapeDtypeStruct((b, t, ATT_HEADS * ATT_DH), F32),
        scratch_shapes=[pltpu.VMEM((2, tq, LANE), F32), pltpu.VMEM((2, tq, 1), F32)],
        compiler_params=_params("parallel", "parallel", "arbitrary"),
    )(proj, proj, proj)


def _sb_sample_kernel(pt_ref, l_ref, q_ref, kn_ref, vn_ref, acc0_ref, carry0_ref, kp_ref, vp_ref, o_ref, co_ref,
                      acc_ref, carry_ref, *, with_new):
    jj = pl.program_id(1)
    nq = q_ref.shape[0] // ATT_HEADS
    upper = _upper_ones(PAGE_SIZE)
    qf = q_ref[...] * (ATT_DH ** -0.5)
    qh = [qf[nq * h:nq * (h + 1)].astype(BF16) for h in range(ATT_HEADS)]

    def visit(kt_ref, vt_ref, earlier):
        z = jnp.concatenate([_dot(qh[h], kt_ref[h].astype(BF16)) for h in range(ATT_HEADS)], axis=0)
        a, carry = _sb_tile(z, earlier, carry_ref[...], upper)
        acc_ref[...] += jnp.concatenate(
            [_dot_nt(a[nq * h:nq * (h + 1)].astype(BF16), vt_ref[h].astype(BF16)) for h in range(ATT_HEADS)], axis=0)
        carry_ref[...] = carry

    @pl.when(jj == 0)
    def _():
        acc_ref[...] = acc0_ref[...]
        carry_ref[...] = carry0_ref[...]
        if with_new:
            rows = _iota((ATT_HEADS * nq, PAGE_SIZE), 0) & (nq - 1)
            visit(kn_ref, vn_ref, _iota((ATT_HEADS * nq, PAGE_SIZE), 1) < rows)

    @pl.when(jnp.logical_not(_all_done(carry_ref[...])))
    def _():
        visit(kp_ref, vp_ref, None)

    @pl.when(jj == pl.num_programs(1) - 1)
    def _():
        o_ref[...] = acc_ref[...]
        co_ref[...] = carry_ref[...]


def _sb_sample_pages(q, kn_t, vn_t, acc0, carry0, cache_k, cache_v, page_table, layer, first_page, n_visit, with_new):
    b, rows, _ = q.shape
    page = lambda i, jj, pt, l: (l[0], pt[i, first_page - jj], 0, 0, 0)
    per_b3 = lambda i, jj, pt, l: (i, 0, 0)
    per_b4 = lambda i, jj, pt, l: (i, 0, 0, 0)
    kv_t = (None, ATT_HEADS, ATT_DH, PAGE_SIZE)
    grid_spec = pltpu.PrefetchScalarGridSpec(
        num_scalar_prefetch=2,
        grid=(b, n_visit),
        in_specs=[
            pl.BlockSpec((None, rows, ATT_DH), per_b3),
            pl.BlockSpec(kv_t, per_b4),
            pl.BlockSpec(kv_t, per_b4),
            pl.BlockSpec((None, rows, ATT_DH), per_b3),
            pl.BlockSpec((None, rows, 1), per_b3),
            pl.BlockSpec((None,) + kv_t, page),
            pl.BlockSpec((None,) + kv_t, page),
        ],
        out_specs=[pl.BlockSpec((None, rows, ATT_DH), per_b3), pl.BlockSpec((None, rows, 1), per_b3)],
        scratch_shapes=[pltpu.VMEM((rows, ATT_DH), F32), pltpu.VMEM((rows, 1), F32)],
    )
    return pl.pallas_call(
        functools.partial(_sb_sample_kernel, with_new=with_new),
        grid_spec=grid_spec,
        out_shape=[jax.ShapeDtypeStruct((b, rows, ATT_DH), F32), jax.ShapeDtypeStruct((b, rows, 1), F32)],
        compiler_params=_params("parallel", "arbitrary"),
    )(page_table, layer, q, kn_t, vn_t, acc0, carry0, cache_k, cache_v)


def _sb_sample(q, kn_t, vn_t, cache_k, cache_v, page_table, layer):
    b, rows, _ = q.shape
    n_pages = page_table.shape[1]
    first = min(SB_FIRST_PAGES, n_pages)
    acc, carry = _sb_sample_pages(q, kn_t, vn_t, jnp.zeros((b, rows, ATT_DH), F32), jnp.zeros((b, rows, 1), F32),
                                  cache_k, cache_v, page_table, layer, n_pages - 1, first, True)
    if first == n_pages:
        return acc
    older = lambda args: _sb_sample_pages(q, kn_t, vn_t, args[0], args[1], cache_k, cache_v, page_table, layer,
                                          n_pages - 1 - first, n_pages - first, False)[0]
    return lax.cond(jnp.max(carry) > -SB_DONE, older, lambda args: args[0], (acc, carry))


def _select_bias(scores, own):
    blk = _iota(scores.shape, 1)
    s = jnp.where(blk < own, scores, -jnp.inf)
    keep = blk == own
    for _ in range(MOBA_TOPK):
        m = jnp.max(s, axis=1, keepdims=True)
        is_max = (s == m) & (m > -jnp.inf)
        first = jnp.min(jnp.where(is_max, blk, LANE), axis=1, keepdims=True)
        pick = blk == first
        keep = keep | pick
        s = jnp.where(pick, -jnp.inf, s)
    return jnp.where(keep, 0.0, NEG_BIAS)


def _kmean_prompt_kernel(k_ref, o_ref):
    o_ref[...] = jnp.mean(k_ref[...], axis=0, keepdims=True)


def _kmean_prompt(proj):
    b, t, _ = proj.shape
    nb = t // MOBA_BLOCK
    w = ATT_HEADS * ATT_DH
    return pl.pallas_call(
        _kmean_prompt_kernel,
        grid=(b, nb),
        in_specs=[pl.BlockSpec((None, MOBA_BLOCK, w), lambda i, n: (i, n, OFF_MBK // w))],
        out_specs=pl.BlockSpec((None, None, 1, w), lambda i, n: (i, n, 0, 0)),
        out_shape=jax.ShapeDtypeStruct((b, nb, 1, w), F32),
        compiler_params=_params("parallel", "parallel"),
    )(proj)


def _select_prompt_kernel(q_ref, km_ref, o_ref):
    i = pl.program_id(2)
    tq = q_ref.shape[0]
    q = q_ref[...]
    km = km_ref[...]
    lane = _iota((tq, LANE), 1)
    own = (i * tq + _iota((tq, 1), 0)) >> 8
    for hh in range(2):
        qm = jnp.where((lane >= ATT_DH) == (hh == 1), q, 0.0)
        scores = lax.dot_general(qm, km, NT_DIMS, precision=lax.Precision.HIGHEST, preferred_element_type=F32)
        o_ref[hh] = _select_bias(scores, own).astype(BF16)


def _select_prompt(proj, kmean):
    b, t, _ = proj.shape
    tq = min(ATT_TILE, t)
    return pl.pallas_call(
        _select_prompt_kernel,
        grid=(b, ATT_HEADS // 2, t // tq),
        in_specs=[
            pl.BlockSpec((None, tq, LANE), lambda i, p, n: (i, n, OFF_MBQ // LANE + p)),
            pl.BlockSpec((None, LANE, LANE), lambda i, p, n: (i, 0, p)),
        ],
        out_specs=pl.BlockSpec((None, 2, tq, LANE), lambda i, p, n: (i, p, n, 0)),
        out_shape=jax.ShapeDtypeStruct((b, ATT_HEADS, t, LANE), BF16),
        compiler_params=_params("parallel", "parallel", "parallel"),
    )(proj, kmean)


def _softmax_step(s, v, m_prev, l_prev, acc_prev):
    m_new = jnp.maximum(m_prev, jnp.max(s, axis=1, keepdims=True))
    p = jnp.exp(s - m_new)
    alpha = jnp.exp(m_prev - m_new)
    return m_new, alpha * l_prev + jnp.sum(p, axis=1, keepdims=True), alpha * acc_prev + _dot(p.astype(BF16), v)


def _moba_prompt_kernel(q_ref, k_ref, v_ref, bias_ref, o_ref, acc_ref, m_ref, l_ref):
    i = pl.program_id(2)
    tq = q_ref.shape[0]
    tk = tq
    acc_ref[...] = jnp.zeros(acc_ref.shape, F32)
    m_ref[...] = jnp.full(m_ref.shape, -jnp.inf, F32)
    l_ref[...] = jnp.zeros(l_ref.shape, F32)
    q2 = q_ref[...] * (ATT_DH ** -0.5)
    lane = _iota((tq, LANE), 1)
    q_aug = [jnp.concatenate([jnp.where((lane >= ATT_DH) == (hh == 1), q2, 0.0).astype(BF16), bias_ref[hh]], axis=1)
             for hh in range(2)]

    def tile(j, visible):
        off = pl.multiple_of(j * tk, tk)
        onehot = (_iota((tk, LANE), 1) == j).astype(BF16)
        k_aug = jnp.concatenate([k_ref[pl.ds(off, tk), :].astype(BF16), onehot], axis=1)
        v2 = v_ref[pl.ds(off, tk), :].astype(BF16)
        for hh in range(2):
            s = _dot_nt(q_aug[hh], k_aug)
            if visible is not None:
                s = jnp.where(visible, s, -jnp.inf)
            m_ref[hh], l_ref[hh], acc_ref[hh] = _softmax_step(s, v2, m_ref[hh], l_ref[hh], acc_ref[hh])

    tile(i, _iota((tq, tk), 1) <= _iota((tq, tk), 0))

    def body(t, c):
        tile(i - 1 - t, None)
        return c

    lax.fori_loop(0, i, body, 0)
    o_ref[...] = jnp.where(lane < ATT_DH, acc_ref[0] / l_ref[0], acc_ref[1] / l_ref[1])


def _moba_prompt(proj, bias):
    b, t, _ = proj.shape
    tq = min(ATT_TILE, t)
    assert tq == MOBA_BLOCK, "the prompt MoBA kernel visits one key block per tile"
    return pl.pallas_call(
        _moba_prompt_kernel,
        grid=(b, ATT_HEADS // 2, t // tq),
        in_specs=[
            pl.BlockSpec((None, tq, LANE), lambda i, p, n: (i, n, OFF_MBQ // LANE + p)),
            pl.BlockSpec((None, t, LANE), lambda i, p, n: (i, 0, OFF_MBK // LANE + p)),
            pl.BlockSpec((None, t, LANE), lambda i, p, n: (i, 0, OFF_MBV // LANE + p)),
            pl.BlockSpec((None, 2, tq, LANE), lambda i, p, n: (i, p, n, 0)),
        ],
        out_specs=pl.BlockSpec((None, tq, LANE), lambda i, p, n: (i, n, p)),
        out_shape=jax.ShapeDtypeStruct((b, t, ATT_HEADS * ATT_DH), F32),
        scratch_shapes=[pltpu.VMEM((2, tq, LANE), F32), pltpu.VMEM((2, tq, 1), F32), pltpu.VMEM((2, tq, 1), F32)],
        compiler_params=_params("parallel", "parallel", "arbitrary"),
    )(proj, proj, proj, bias)


def _put_lane(ref, n, col):
    ref[...] = jnp.where(_iota(ref.shape, 1) == n, col, ref[...])


def _moba_sample_kernel(pt_ref, l_ref, q_ref, kn_ref, vn_ref, k0_ref, k1_ref, v0_ref, v1_ref, o_ref,
                        score_ref, m_ref, l_blk_ref, o_blk_ref, own_m_ref, own_l_ref, own_o_ref, *, own):
    n = pl.program_id(1)
    nq = q_ref.shape[0] // ATT_HEADS
    rows = ATT_HEADS * nq
    qf = q_ref[...]
    qh = [qf[nq * h:nq * (h + 1)].astype(BF16) for h in range(ATT_HEADS)]
    scale = ATT_DH ** -0.5

    def logits(kt_refs):
        return jnp.concatenate(
            [jnp.concatenate([_dot(qh[h], kt[h].astype(BF16)) for kt in kt_refs], axis=1)
             for h in range(ATT_HEADS)], axis=0)

    def partials(s, vt_refs):
        m = jnp.max(s, axis=1, keepdims=True)
        p = jnp.exp(s - m)
        o = None
        for c, vt in enumerate(vt_refs):
            oc = jnp.concatenate(
                [_dot_nt(p[nq * h:nq * (h + 1), PAGE_SIZE * c:PAGE_SIZE * (c + 1)].astype(BF16), vt[h].astype(BF16))
                 for h in range(ATT_HEADS)], axis=0)
            o = oc if o is None else o + oc
        return m, jnp.sum(p, axis=1, keepdims=True), o

    @pl.when(n == 0)
    def _():
        score_ref[...] = jnp.zeros(score_ref.shape, F32)
        m_ref[...] = jnp.zeros(m_ref.shape, F32)
        l_blk_ref[...] = jnp.zeros(l_blk_ref.shape, F32)
        qrow = _iota((rows, PAGE_SIZE), 0) & (nq - 1)
        s = jnp.where(_iota((rows, PAGE_SIZE), 1) <= qrow, logits([kn_ref]) * scale, -jnp.inf)
        own_m_ref[...], own_l_ref[...], own_o_ref[...] = partials(s, [vn_ref])

    z = logits([k0_ref, k1_ref])
    _put_lane(score_ref, n, jnp.sum(z, axis=1, keepdims=True) * (1.0 / MOBA_BLOCK))
    m, l, o = partials(z * scale, [v0_ref, v1_ref])
    _put_lane(m_ref, n, m)
    _put_lane(l_blk_ref, n, l)
    o_blk_ref[n] = o

    @pl.when(n == pl.num_programs(1) - 1)
    def _():
        keep = _select_bias(score_ref[...], jnp.full((rows, 1), own, jnp.int32)) == 0.0
        keep = keep & (_iota((rows, LANE), 1) < own)
        m_own, l_own = own_m_ref[...], own_l_ref[...]
        m_blk = m_ref[...]
        m_all = jnp.maximum(m_own, jnp.max(jnp.where(keep, m_blk, -jnp.inf), axis=1, keepdims=True))
        w = jnp.where(keep, jnp.exp(m_blk - m_all), 0.0)
        w_own = jnp.exp(m_own - m_all)
        l_all = w_own * l_own + jnp.sum(w * l_blk_ref[...], axis=1, keepdims=True)
        acc = w_own * own_o_ref[...]
        for blk in range(own):
            acc = acc + w[:, blk:blk + 1] * o_blk_ref[blk]
        o_ref[...] = acc / l_all


def _moba_sample(q, kn_t, vn_t, cache_k, cache_v, page_table, layer):
    b, rows, _ = q.shape
    n_pages = page_table.shape[1]
    ppb = MOBA_BLOCK // PAGE_SIZE
    assert ppb == 2 and n_pages % ppb == 0 and n_pages // ppb < LANE
    n_blocks = n_pages // ppb
    page = lambda c: (lambda i, n, pt, l: (l[0], pt[i, ppb * n + c], 0, 0, 0))
    per_b3 = lambda i, n, pt, l: (i, 0, 0)
    per_b4 = lambda i, n, pt, l: (i, 0, 0, 0)
    kv_t = (None, ATT_HEADS, ATT_DH, PAGE_SIZE)
    grid_spec = pltpu.PrefetchScalarGridSpec(
        num_scalar_prefetch=2,
        grid=(b, n_blocks),
        in_specs=[
            pl.BlockSpec((None, rows, ATT_DH), per_b3),
            pl.BlockSpec(kv_t, per_b4),
            pl.BlockSpec(kv_t, per_b4),
            pl.BlockSpec((None,) + kv_t, page(0)),
            pl.BlockSpec((None,) + kv_t, page(1)),
            pl.BlockSpec((None,) + kv_t, page(0)),
            pl.BlockSpec((None,) + kv_t, page(1)),
        ],
        out_specs=pl.BlockSpec((None, rows, ATT_DH), per_b3),
        scratch_shapes=[pltpu.VMEM((rows, LANE), F32), pltpu.VMEM((rows, LANE), F32), pltpu.VMEM((rows, LANE), F32),
                        pltpu.VMEM((n_blocks, rows, ATT_DH), F32), pltpu.VMEM((rows, 1), F32),
                        pltpu.VMEM((rows, 1), F32), pltpu.VMEM((rows, ATT_DH), F32)],
    )
    return pl.pallas_call(
        functools.partial(_moba_sample_kernel, own=n_blocks),
        grid_spec=grid_spec,
        out_shape=jax.ShapeDtypeStruct((b, rows, ATT_DH), F32),
        compiler_params=_params("parallel", "arbitrary"),
    )(page_table, layer, q, kn_t, vn_t, cache_k, cache_k, cache_v, cache_v)


def _merge_kernel(o0_ref, o1_ref, o2_ref, o3_ref, g0_ref, g1_ref, g2_ref, g3_ref, x_ref, wb_ref, wo_ref, nw_ref,
                  x1_ref, hf_ref):
    mixed = None
    for o_ref, g_ref, i in ((o0_ref, g0_ref, 0), (o1_ref, g1_ref, 1), (o2_ref, g2_ref, 2), (o3_ref, g3_ref, 3)):
        term = _sigmoid(g_ref[...]) * _dot(o_ref[...].astype(BF16), wb_ref[i])
        mixed = term if mixed is None else mixed + term
    x1 = x_ref[...] + _dot(mixed.astype(BF16), wo_ref[...])
    x1_ref[...] = x1
    ms = jnp.mean(x1 * x1, axis=-1, keepdims=True)
    hf_ref[...] = (x1 * lax.rsqrt(ms + EPS) * nw_ref[...]).astype(BF16)


def _merge(branches, proj, x, lp):
    n, d = x.shape
    tm = _tile(n, 256)
    row = lambda i: (i, 0)
    gate = lambda g: (lambda i: (i, OFF_GATE // d + g))
    return pl.pallas_call(
        _merge_kernel,
        grid=(n // tm,),
        in_specs=[pl.BlockSpec((tm, BRANCH_WIDTH), row)] * 4
        + [pl.BlockSpec((tm, d), gate(g)) for g in range(4)]
        + [
            pl.BlockSpec((tm, d), row),
            pl.BlockSpec((4, BRANCH_WIDTH, d), lambda i: (0, 0, 0)),
            pl.BlockSpec((d, d), lambda i: (0, 0)),
            pl.BlockSpec((1, d), lambda i: (0, 0)),
        ],
        out_specs=[pl.BlockSpec((tm, d), row), pl.BlockSpec((tm, d), row)],
        out_shape=[jax.ShapeDtypeStruct((n, d), F32), jax.ShapeDtypeStruct((n, d), BF16)],
        compiler_params=_params("parallel"),
    )(*branches, proj, proj, proj, proj, x, lp["w_branch"], lp["w_out"], lp["norm_ffn"])


def _ffn_kernel(u_ref, halo_ref, halo0_ref, x1_ref, cw_ref, cb_ref, wd_ref, o_ref, ext_ref, *, tiles_per_seq):
    i = pl.program_id(0)
    tm = u_ref.shape[0]
    seq_start = (i % tiles_per_seq) == 0

    @pl.when(seq_start)
    def _():
        ext_ref[0:SUBLANE, :] = halo0_ref[...]

    @pl.when(jnp.logical_not(seq_start))
    def _():
        ext_ref[0:SUBLANE, :] = halo_ref[...]

    u = u_ref[...]
    ext_ref[SUBLANE:SUBLANE + tm, :] = u
    cw = cw_ref[...]
    conv = u * cw[FFN_CONV - 1:FFN_CONV] + cb_ref[...]
    for t in range(FFN_CONV - 1):
        off = SUBLANE - (FFN_CONV - 1) + t
        conv = conv + ext_ref[off:off + tm, :] * cw[t:t + 1]
    act = _silu(conv[:, :D_FF]) * conv[:, D_FF:]
    o_ref[...] = x1_ref[...] + _dot(act.astype(BF16), wd_ref[...])


def _ffn(u, halo0, x1, seq_len, lp):
    n, w = u.shape
    d = x1.shape[1]
    tm = _tile(seq_len, 256)
    tiles_per_seq = seq_len // tm
    hb = tm // SUBLANE
    return pl.pallas_call(
        functools.partial(_ffn_kernel, tiles_per_seq=tiles_per_seq),
        grid=(n // tm,),
        in_specs=[
            pl.BlockSpec((tm, w), lambda i: (i, 0)),
            pl.BlockSpec((SUBLANE, w), lambda i: (jnp.maximum(i * hb - 1, 0), 0)),
            pl.BlockSpec((None, SUBLANE, w), lambda i: (i // tiles_per_seq, 0, 0)),
            pl.BlockSpec((tm, d), lambda i: (i, 0)),
            pl.BlockSpec((FFN_CONV, w), lambda i: (0, 0)),
            pl.BlockSpec((1, w), lambda i: (0, 0)),
            pl.BlockSpec((D_FF, d), lambda i: (0, 0)),
        ],
        out_specs=pl.BlockSpec((tm, d), lambda i: (i, 0)),
        out_shape=jax.ShapeDtypeStruct((n, d), F32),
        scratch_shapes=[pltpu.VMEM((tm + SUBLANE, w), F32)],
        compiler_params=_params("parallel"),
    )(u, u, halo0, x1, lp["ffn_conv_w"], lp["ffn_conv_b"], lp["w_down"])


def _rope_tables(pos0, t):
    half = RET_DK // 2
    inv = ROPE_BASE ** (-jnp.arange(half, dtype=F32) / half)
    ang = (pos0 + jnp.arange(t, dtype=jnp.int32)).astype(F32)[:, None] * inv[None, :]
    cos, sin = jnp.cos(ang), jnp.sin(ang)
    cos_t = jnp.tile(jnp.concatenate([cos, cos], axis=1), (1, RET_HEADS))
    sin_t = jnp.tile(jnp.concatenate([-sin, sin], axis=1), (1, RET_HEADS))
    return cos_t, sin_t


def _pad_state_rows(s):
    return jnp.pad(s, ((0, 0), (SUBLANE - s.shape[1], 0), (0, 0)))


def _ssm_state_in(h0):
    b = h0.shape[0]
    h2 = h0.reshape(b, SSM_INNER, SSM_STATE)
    g0 = (jnp.arange(SSM_INNER) < SSM_INNER // SSM_GROUPS)[None, :, None]
    return jnp.concatenate([jnp.where(g0, h2, 0.0), jnp.where(g0, 0.0, h2)], axis=-1)


def _ssm_state_out(h2):
    b = h2.shape[0]
    g0 = (jnp.arange(SSM_INNER) < SSM_INNER // SSM_GROUPS)[None, :, None]
    return jnp.where(g0, h2[..., :SSM_STATE], h2[..., SSM_STATE:]).reshape(b, SSM_HEADS, SSM_HEADDIM, SSM_STATE)


def _prep_layer_weights(w):
    depth = w["w_in"].shape[0]
    w_in = w["w_in"]
    e_xbc = OFF_Z + SSM_INNER + SSM_CONV_DIM
    w_in = jnp.concatenate([
        w_in[..., :OFF_Z + SSM_INNER],
        w_in[..., e_xbc:e_xbc + SSM_HEADS],
        jnp.zeros(w_in.shape[:2] + (DT_PAD - SSM_HEADS,), w_in.dtype),
        w_in[..., OFF_Z + SSM_INNER:e_xbc],
        w_in[..., e_xbc + SSM_HEADS:],
    ], axis=-1)
    assert w_in.shape[-1] == P_TOTAL
    pad_heads = lambda a: jnp.pad(a, ((0, 0), (0, LANE - SSM_HEADS)))[:, None, :]
    return {
        "norm_mix": w["norm_mix"],
        "w_in": w_in.astype(BF16),
        "ssm_conv_w": w["ssm_conv_w"],
        "ssm_conv_b": w["ssm_conv_b"][:, None, :],
        "ssm_dt_bias": pad_heads(w["ssm_dt_bias"]),
        "ssm_a_log": pad_heads(w["ssm_a_log"]),
        "ssm_d": jnp.repeat(w["ssm_d"], SSM_HEADDIM, axis=1)[:, None, :],
        "ssm_norm": w["ssm_norm"][:, None, :],
        "w_branch": w["w_branch"].astype(BF16),
        "w_out": w["w_out"].astype(BF16),
        "norm_ffn": w["norm_ffn"][:, None, :],
        "w_up": w["w_up"].astype(BF16),
        "ffn_conv_w": w["ffn_conv_w"],
        "ffn_conv_b": w["ffn_conv_b"][:, None, :],
        "w_down": w["w_down"].astype(BF16),
        "layer": jnp.arange(depth, dtype=jnp.int32)[:, None],
    }


def _heads_major(a, b, t):
    return a.reshape(b, t, ATT_HEADS, ATT_DH).transpose(0, 2, 1, 3)


def _layer(x, lp, past, pos0, caches, page_table):
    b, t, d = x.shape
    n = b * t
    r0, h0, conv0, ffn0 = past
    x2d = x.reshape(n, d)
    hn = _rmsnorm(x2d, lp["norm_mix"], BF16)
    proj = _matmul(hn, lp["w_in"])
    proj3 = proj.reshape(b, t, P_TOTAL)
    w_att = ATT_HEADS * ATT_DH

    cos, sin = _rope_tables(pos0, t)
    o_ret, r_new = _retention(proj3, cos, sin, r0)
    o_ssm, conv_new, h_new = _ssd(proj3, _pad_state_rows(conv0), _ssm_state_in(h0), lp)

    sbk, sbv = proj[:, OFF_SBK:OFF_SBK + w_att], proj[:, OFF_SBV:OFF_SBV + w_att]
    mbk, mbv = proj[:, OFF_MBK:OFF_MBK + w_att], proj[:, OFF_MBV:OFF_MBV + w_att]
    if caches is None:
        o_sb = _sb_prompt(proj3)
        kmean = _kmean_prompt(proj3)[:, :, 0, :]
        kmean = jnp.pad(kmean, ((0, 0), (0, LANE - kmean.shape[1]), (0, 0)))
        o_mb = _moba_prompt(proj3, _select_prompt(proj3, kmean))
    else:
        csk, csv, cmk, cmv = caches
        layer = lp["layer"]
        rows = ATT_HEADS * t
        new_t = lambda a: jnp.pad(a.reshape(b, t, ATT_HEADS, ATT_DH).transpose(0, 2, 3, 1),
                                  ((0, 0), (0, 0), (0, 0), (0, PAGE_SIZE - t)))
        to_rows = lambda a: _heads_major(a, b, t).reshape(b, rows, ATT_DH)
        from_rows = lambda o: o.reshape(b, ATT_HEADS, t, ATT_DH).transpose(0, 2, 1, 3).reshape(b, t, w_att)
        sq = to_rows(proj[:, OFF_SBQ:OFF_SBQ + w_att])
        o_sb = from_rows(_sb_sample(sq, new_t(sbk), new_t(sbv), csk, csv, page_table, layer))
        mq = to_rows(proj[:, OFF_MBQ:OFF_MBQ + w_att])
        o_mb = from_rows(_moba_sample(mq, new_t(mbk), new_t(mbv), cmk, cmv, page_table, layer))

    branches = [o.reshape(n, BRANCH_WIDTH) for o in (o_ret, o_sb, o_mb, o_ssm)]
    x1, hf = _merge(branches, proj, x2d, lp)
    u = _matmul(hf, lp["w_up"])
    x2 = _ffn(u, _pad_state_rows(ffn0), x1, t, lp)
    heads = lambda a: a.reshape(b, t, ATT_HEADS, ATT_DH)
    state = (heads(sbk), heads(sbv), heads(mbk), heads(mbv), r_new, _ssm_state_out(h_new),
             conv_new[:, SUBLANE - (SSM_CONV - 1):], u.reshape(b, t, -1)[:, t - (FFN_CONV - 1):])
    return x2.reshape(b, t, d), state


def _trunk(x, pos0, past, lw, norm_final, caches, page_table):
    def body(carry, per_layer):
        lp, pst = per_layer
        y, st = _layer(carry, lp, pst, pos0, caches, page_table)
        return y, st

    y, states = lax.scan(body, x, (lw, past))
    b, t, d = y.shape
    out = _rmsnorm(y.reshape(b * t, d), norm_final, F32).reshape(b, t, d)
    return out, states


def kernel(x_prompt, x_sample, cache_sb_k, cache_sb_v, cache_moba_k, cache_moba_v, page_table, state_ret, state_ssm,
           state_ssm_conv, state_ffn_conv, norm_mix, w_in, ssm_conv_w, ssm_conv_b, ssm_dt_bias, ssm_a_log, ssm_d,
           ssm_norm, w_branch, w_out, norm_ffn, w_up, ffn_conv_w, ffn_conv_b, w_down, norm_final):
    lw = _prep_layer_weights({
        "norm_mix": norm_mix, "w_in": w_in, "ssm_conv_w": ssm_conv_w, "ssm_conv_b": ssm_conv_b,
        "ssm_dt_bias": ssm_dt_bias, "ssm_a_log": ssm_a_log, "ssm_d": ssm_d, "ssm_norm": ssm_norm,
        "w_branch": w_branch, "w_out": w_out, "norm_ffn": norm_ffn, "w_up": w_up, "ffn_conv_w": ffn_conv_w,
        "ffn_conv_b": ffn_conv_b, "w_down": w_down})
    depth = w_in.shape[0]
    bp = x_prompt.shape[0]
    zeros = lambda *s: jnp.zeros((depth, bp) + s, F32)
    past_p = (zeros(RET_HEADS, RET_DK, RET_DV), zeros(SSM_HEADS, SSM_HEADDIM, SSM_STATE),
              zeros(SSM_CONV - 1, SSM_CONV_DIM), zeros(FFN_CONV - 1, 2 * D_FF))
    y_p, st_p = _trunk(x_prompt, 0, past_p, lw, norm_final, None, None)

    caches = tuple(c.transpose(0, 1, 3, 4, 2) for c in (cache_sb_k, cache_sb_v, cache_moba_k, cache_moba_v))
    past_s = (state_ret, state_ssm, state_ssm_conv, state_ffn_conv)
    pos0 = page_table.shape[1] * PAGE_SIZE
    y_s, st_s = _trunk(x_sample, pos0, past_s, lw, norm_final, caches, page_table)
    return (y_p, y_s) + tuple(st_p) + tuple(st_s)
```

```python
import functools
import math

import jax
import jax.numpy as jnp
from jax import lax
from jax.experimental import pallas as pl
from jax.experimental.pallas import tpu as pltpu

F32 = jnp.float32
BF16 = jnp.bfloat16

EPS = 1e-6
D_MODEL = 1024
BRANCH_WIDTH = 512
RET_HEADS, RET_DK, RET_DV = 4, 64, 128
ROPE_BASE = 10000.0
ATT_HEADS, ATT_DH = 8, 64
MOBA_BLOCK, MOBA_TOPK = 256, 3
SSM_HEADS, SSM_HEADDIM, SSM_STATE, SSM_GROUPS, SSM_CONV = 8, 64, 64, 2, 4
SSM_INNER = SSM_HEADS * SSM_HEADDIM
SSM_CONV_DIM = SSM_INNER + 2 * SSM_GROUPS * SSM_STATE
D_FF = 2816
FFN_CONV = 3
PAGE_SIZE = 128
CHUNK = 128
ATT_TILE = 256
LOG_GAMMA = tuple(math.log1p(-2.0 ** (-5.0 - h)) for h in range(RET_HEADS))

OFF_RQ, OFF_RK, OFF_RV, OFF_RG = 0, 256, 512, 1024
OFF_SBQ, OFF_SBK, OFF_SBV = 1536, 2048, 2560
OFF_MBQ, OFF_MBK, OFF_MBV = 3072, 3584, 4096
OFF_Z, OFF_DT, OFF_XBC, OFF_GATE = 4608, 5120, 5376, 6144
DT_PAD = 256
P_TOTAL = OFF_GATE + 4 * D_MODEL
NEG_BIAS = -1e30
SB_DONE = 104.0
SB_FIRST_PAGES = 4

LANE = 128
SUBLANE = 8
VMEM_LIMIT = 56 * 1024 * 1024

NT_DIMS = (((1,), (1,)), ((), ()))
TN_DIMS = (((0,), (0,)), ((), ()))


def _dot(a, b):
    return jnp.dot(a, b, preferred_element_type=F32)


def _dot_nt(a, b):
    return lax.dot_general(a, b, NT_DIMS, preferred_element_type=F32)


def _dot_tn(a, b):
    return lax.dot_general(a, b, TN_DIMS, preferred_element_type=F32)


def _split_bf16(x, n):
    terms, r = [], x
    for _ in range(n):
        t = r.astype(BF16)
        terms.append(t)
        r = r - t.astype(F32)
    return terms


def _dot_split_rhs(a01, x, n=3):
    out = None
    for t in _split_bf16(x, n):
        d = _dot(a01, t)
        out = d if out is None else out + d
    return out


def _dot_split_lhs(x, a01, n=3):
    out = None
    for t in _split_bf16(x, n):
        d = _dot(t, a01)
        out = d if out is None else out + d
    return out


def _iota(shape, dim):
    return lax.broadcasted_iota(jnp.int32, shape, dim)


def _sigmoid(x):
    return 1.0 / (1.0 + jnp.exp(-x))


def _silu(x):
    return x * _sigmoid(x)


def _log_sigmoid(x):
    return jnp.minimum(x, 0.0) - jnp.log1p(jnp.exp(-jnp.abs(x)))


def _softplus(x):
    return jnp.maximum(x, 0.0) + jnp.log1p(jnp.exp(-jnp.abs(x)))


def _pad_rows(x, rows):
    if x.shape[0] == rows:
        return x
    return jnp.concatenate([x, jnp.zeros((rows - x.shape[0],) + x.shape[1:], x.dtype)], axis=0)


def _tile(n, cap, mult=SUBLANE):
    if n <= cap:
        return n
    for t in range(cap - cap % mult, 0, -mult):
        if n % t == 0:
            return t
    raise ValueError(f"no tile for {n}")


def _params(*sem):
    return pltpu.CompilerParams(dimension_semantics=sem, vmem_limit_bytes=VMEM_LIMIT)


def _rmsnorm_kernel(x_ref, w_ref, o_ref):
    x = x_ref[...]
    ms = jnp.mean(x * x, axis=-1, keepdims=True)
    o_ref[...] = (x * lax.rsqrt(ms + EPS) * w_ref[...]).astype(o_ref.dtype)


def _rmsnorm(x, w, out_dtype):
    n, d = x.shape
    tm = _tile(n, 512)
    return pl.pallas_call(
        _rmsnorm_kernel,
        grid=(n // tm,),
        in_specs=[pl.BlockSpec((tm, d), lambda i: (i, 0)), pl.BlockSpec((1, d), lambda i: (0, 0))],
        out_specs=pl.BlockSpec((tm, d), lambda i: (i, 0)),
        out_shape=jax.ShapeDtypeStruct((n, d), out_dtype),
        compiler_params=_params("parallel"),
    )(x, w.reshape(1, d))


def _matmul_kernel(x_ref, w_ref, o_ref):
    o_ref[...] = _dot(x_ref[...], w_ref[...])


def _matmul(x, w):
    n, k = x.shape
    m = w.shape[1]
    tm = _tile(n, 1024)
    tn = _tile(m, 1536, LANE)
    return pl.pallas_call(
        _matmul_kernel,
        grid=(m // tn, n // tm),
        in_specs=[pl.BlockSpec((tm, k), lambda c, r: (r, 0)), pl.BlockSpec((k, tn), lambda c, r: (0, c))],
        out_specs=pl.BlockSpec((tm, tn), lambda c, r: (r, c)),
        out_shape=jax.ShapeDtypeStruct((n, m), F32),
        compiler_params=_params("parallel", "parallel"),
    )(x, w)


def _retention_kernel(q_ref, k_ref, v_ref, g_ref, cos_ref, sin_ref, r0_ref, o_ref, rn_ref, r_ref, *, c, tv):
    n = pl.program_id(1)
    hk, hv = RET_HEADS * RET_DK, RET_HEADS * RET_DV

    @pl.when(n == 0)
    def _():
        r_ref[...] = jnp.zeros((hk, hv), F32)
        for h in range(RET_HEADS):
            r_ref[RET_DK * h:RET_DK * (h + 1), RET_DV * h:RET_DV * (h + 1)] = r0_ref[h]

    rows = q_ref.shape[0]
    q, k, v = _pad_rows(q_ref[...], c), _pad_rows(k_ref[...], c), _pad_rows(v_ref[...], c)
    cos, sin = _pad_rows(cos_ref[...], c), _pad_rows(sin_ref[...], c)
    lane = _iota((c, hk), 1)
    head_of_lane = lane >> 6
    first_half = (_iota((c, LANE), 1) & 63) < 32

    def rope(x):
        parts = []
        for s in range(hk // LANE):
            xs = x[:, LANE * s:LANE * (s + 1)]
            parts.append(jnp.where(first_half, pltpu.roll(xs, LANE - 32, 1), pltpu.roll(xs, 32, 1)))
        return x * cos + jnp.concatenate(parts, axis=1) * sin

    qr = rope(q) * (RET_DK ** -0.5)
    kr = rope(k)
    lg = jnp.full((c, hk), LOG_GAMMA[RET_HEADS - 1], F32)
    for h in range(RET_HEADS - 2, -1, -1):
        lg = jnp.where(head_of_lane == h, LOG_GAMMA[h], lg)
    ri = _iota((c, hk), 0).astype(F32)
    q_from_start = jnp.exp(lg * (ri + 1.0))
    k_to_end = jnp.exp(lg * (float(tv - 1) - ri))
    kb, vb = kr.astype(BF16), v.astype(BF16)
    di = _iota((c, c), 0) - _iota((c, c), 1)
    dif = jnp.maximum(di, 0).astype(F32)
    inner = []
    for h in range(RET_HEADS):
        decay = jnp.where(di >= 0, jnp.exp(LOG_GAMMA[h] * dif), 0.0)
        qm = jnp.where(head_of_lane == h, qr, 0.0).astype(BF16)
        s = _dot_nt(qm, kb)
        inner.append(_dot((s * decay).astype(BF16), vb[:, RET_DV * h:RET_DV * (h + 1)]))
    r = r_ref[...]
    cross = _dot((qr * q_from_start).astype(BF16), r.astype(BF16))
    ret = jnp.concatenate(inner, axis=1) + cross
    kv = _dot_tn((kr * k_to_end).astype(BF16), vb)
    row_head = _iota((hk, hv), 0) >> 6
    col_head = _iota((hk, hv), 1) >> 7
    g_chunk = jnp.full((hk, hv), math.exp(LOG_GAMMA[RET_HEADS - 1] * tv), F32)
    for h in range(RET_HEADS - 2, -1, -1):
        g_chunk = jnp.where(row_head == h, math.exp(LOG_GAMMA[h] * tv), g_chunk)
    r_new = g_chunk * r + jnp.where(row_head == col_head, kv, 0.0)
    r_ref[...] = r_new
    g = _pad_rows(g_ref[...], c)
    outs = []
    for h in range(RET_HEADS):
        x = ret[:, RET_DV * h:RET_DV * (h + 1)]
        y = x * lax.rsqrt(jnp.mean(x * x, axis=-1, keepdims=True) + EPS)
        outs.append(y * _silu(g[:, RET_DV * h:RET_DV * (h + 1)]))
    o_ref[...] = jnp.concatenate(outs, axis=1)[:rows]

    @pl.when(n == pl.num_programs(1) - 1)
    def _():
        for h in range(RET_HEADS):
            rn_ref[h] = r_new[RET_DK * h:RET_DK * (h + 1), RET_DV * h:RET_DV * (h + 1)]


def _retention(proj, cos, sin, r0):
    b, t, _ = proj.shape
    cb = min(t, CHUNK)
    hk, hv = RET_HEADS * RET_DK, RET_HEADS * RET_DV
    kern = functools.partial(_retention_kernel, c=CHUNK, tv=cb)
    return pl.pallas_call(
        kern,
        grid=(b, t // cb),
        in_specs=[
            pl.BlockSpec((None, cb, hk), lambda i, n: (i, n, OFF_RQ // hk)),
            pl.BlockSpec((None, cb, hk), lambda i, n: (i, n, OFF_RK // hk)),
            pl.BlockSpec((None, cb, hv), lambda i, n: (i, n, OFF_RV // hv)),
            pl.BlockSpec((None, cb, hv), lambda i, n: (i, n, OFF_RG // hv)),
            pl.BlockSpec((cb, hk), lambda i, n: (n, 0)),
            pl.BlockSpec((cb, hk), lambda i, n: (n, 0)),
            pl.BlockSpec((None, RET_HEADS, RET_DK, RET_DV), lambda i, n: (i, 0, 0, 0)),
        ],
        out_specs=[
            pl.BlockSpec((None, cb, hv), lambda i, n: (i, n, 0)),
            pl.BlockSpec((None, RET_HEADS, RET_DK, RET_DV), lambda i, n: (i, 0, 0, 0)),
        ],
        out_shape=[jax.ShapeDtypeStruct((b, t, hv), F32), jax.ShapeDtypeStruct(r0.shape, F32)],
        scratch_shapes=[pltpu.VMEM((hk, hv), F32)],
        compiler_params=_params("parallel", "arbitrary"),
    )(proj, proj, proj, proj, cos, sin, r0)


def _ssd_kernel(z_ref, dt_ref, xbc_ref, conv0_ref, h0_ref, cw_ref, cb_ref, dtb_ref, alog_ref, dx_ref, nw_ref,
                o_ref, convn_ref, hn_ref, xp_ref, h_ref, *, c, tv):
    n = pl.program_id(1)
    rows = xbc_ref.shape[0]
    half = SSM_INNER // SSM_GROUPS

    @pl.when(n == 0)
    def _():
        xp_ref[0:SUBLANE, :] = conv0_ref[...]
        h_ref[...] = h0_ref[...]

    xraw = _pad_rows(xbc_ref[...], c)
    xp_ref[SUBLANE:SUBLANE + c, :] = xraw
    cw = cw_ref[...]
    acc = xraw * cw[SSM_CONV - 1:SSM_CONV] + cb_ref[...]
    for i in range(SSM_CONV - 1):
        off = SUBLANE - (SSM_CONV - 1) + i
        acc = acc + xp_ref[off:off + c, :] * cw[i:i + 1]
    tail = xp_ref[tv:tv + SUBLANE, :]
    xp_ref[0:SUBLANE, :] = tail
    xbc = _silu(acc)
    xs = xbc[:, :SSM_INNER]
    bm = xbc[:, SSM_INNER:SSM_INNER + LANE]
    cm = xbc[:, SSM_INNER + LANE:SSM_INNER + 2 * LANE]

    dt = _softplus(_pad_rows(dt_ref[...], c) + dtb_ref[...])
    if tv < c:
        dt = jnp.where(_iota((c, LANE), 0) < tv, dt, 0.0)
    da = dt * (-jnp.exp(alog_ref[...]))
    tril = _iota((c, c), 0) >= _iota((c, c), 1)
    cum = _dot_split_rhs(tril.astype(BF16), da)
    cum_t = cum.T
    expand = ((_iota((LANE, SSM_INNER), 1) >> 6) == _iota((LANE, SSM_INNER), 0)).astype(BF16)
    cumx = _dot_split_lhs(cum, expand)
    dtx = _dot_split_lhs(dt, expand)
    to_end = jnp.exp(cumx[tv - 1:tv, :] - cumx) * dtx
    x_to_end = (xs * to_end).astype(BF16)
    x_dt = (xs * dtx).astype(BF16)
    bb = bm.astype(BF16)
    lane = _iota((c, LANE), 1)
    cg = [jnp.where((lane >> 6) == g, cm, 0.0).astype(BF16) for g in range(SSM_GROUPS)]
    scores = [_dot_nt(cg[g], bb) for g in range(SSM_GROUPS)]
    y_pairs = []
    for p in range(SSM_HEADS // 2):
        y_head = []
        for hh in range(2):
            h = 2 * p + hh
            seg = cum[:, h:h + 1] - cum_t[h:h + 1, :]
            m = scores[h // (SSM_HEADS // SSM_GROUPS)] * jnp.exp(jnp.where(tril, seg, -jnp.inf))
            y_head.append(_dot(m.astype(BF16), x_dt[:, LANE * p:LANE * (p + 1)]))
        y_pairs.append(jnp.where(lane < SSM_HEADDIM, y_head[0], y_head[1]))
    hs = h_ref[...]
    hsb = hs.astype(BF16)
    y_off = jnp.concatenate([_dot_nt(cg[g], hsb[half * g:half * (g + 1)]) for g in range(SSM_GROUPS)], axis=1)
    y = jnp.concatenate(y_pairs, axis=1) + y_off * jnp.exp(cumx) + dx_ref[...] * xs
    tz = y * _silu(_pad_rows(z_ref[...], c))
    out = tz * lax.rsqrt(jnp.mean(tz * tz, axis=-1, keepdims=True) + EPS) * nw_ref[...]
    o_ref[...] = out[:rows]

    expand_t = ((_iota((SSM_INNER, LANE), 0) >> 6) == _iota((SSM_INNER, LANE), 1)).astype(BF16)
    cum_last = jnp.broadcast_to(cum_t[:, tv - 1:tv], (LANE, LANE))
    chunk_decay = jnp.exp(_dot_split_rhs(expand_t, cum_last))
    contrib = jnp.concatenate([_dot_tn(x_to_end[:, half * g:half * (g + 1)], bb) for g in range(SSM_GROUPS)], axis=0)
    h_new = chunk_decay * hs + contrib
    h_ref[...] = h_new

    @pl.when(n == pl.num_programs(1) - 1)
    def _():
        hn_ref[...] = h_new
        convn_ref[...] = tail


def _ssd(proj, conv0, h0, lp):
    b, t, _ = proj.shape
    cb = min(t, CHUNK)
    kern = functools.partial(_ssd_kernel, c=CHUNK, tv=cb)
    const = lambda i, n: (0, 0)
    return pl.pallas_call(
        kern,
        grid=(b, t // cb),
        in_specs=[
            pl.BlockSpec((None, cb, SSM_INNER), lambda i, n: (i, n, OFF_Z // SSM_INNER)),
            pl.BlockSpec((None, cb, LANE), lambda i, n: (i, n, OFF_DT // LANE)),
            pl.BlockSpec((None, cb, SSM_CONV_DIM), lambda i, n: (i, n, OFF_XBC // SSM_CONV_DIM)),
            pl.BlockSpec((None, SUBLANE, SSM_CONV_DIM), lambda i, n: (i, 0, 0)),
            pl.BlockSpec((None, SSM_INNER, LANE), lambda i, n: (i, 0, 0)),
            pl.BlockSpec((SSM_CONV, SSM_CONV_DIM), const),
            pl.BlockSpec((1, SSM_CONV_DIM), const),
            pl.BlockSpec((1, LANE), const),
            pl.BlockSpec((1, LANE), const),
            pl.BlockSpec((1, SSM_INNER), const),
            pl.BlockSpec((1, SSM_INNER), const),
        ],
        out_specs=[
            pl.BlockSpec((None, cb, SSM_INNER), lambda i, n: (i, n, 0)),
            pl.BlockSpec((None, SUBLANE, SSM_CONV_DIM), lambda i, n: (i, 0, 0)),
            pl.BlockSpec((None, SSM_INNER, LANE), lambda i, n: (i, 0, 0)),
        ],
        out_shape=[
            jax.ShapeDtypeStruct((b, t, SSM_INNER), F32),
            jax.ShapeDtypeStruct((b, SUBLANE, SSM_CONV_DIM), F32),
            jax.ShapeDtypeStruct((b, SSM_INNER, LANE), F32),
        ],
        scratch_shapes=[pltpu.VMEM((CHUNK + SUBLANE, SSM_CONV_DIM), F32), pltpu.VMEM((SSM_INNER, LANE), F32)],
        compiler_params=_params("parallel", "arbitrary"),
    )(proj, proj, proj, conv0, h0, lp["ssm_conv_w"], lp["ssm_conv_b"], lp["ssm_dt_bias"], lp["ssm_a_log"],
      lp["ssm_d"], lp["ssm_norm"])


def _sb_tile(z, earlier, carry, upper):
    ls = _log_sigmoid(z)
    lk = ls - z
    if earlier is not None:
        lk = jnp.where(earlier, lk, 0.0)
    within = _dot_split_lhs(lk, upper, 2)
    a = jnp.exp(ls + within + carry)
    if earlier is not None:
        a = jnp.where(earlier, a, 0.0)
    return a, carry + jnp.sum(lk, axis=1, keepdims=True)


def _upper_ones(tk):
    return (_iota((tk, tk), 0) > _iota((tk, tk), 1)).astype(BF16)


def _all_done(carry):
    c = carry if not isinstance(carry, (list, tuple)) else functools.reduce(jnp.maximum, carry)
    return jnp.max(c, axis=0, keepdims=True)[0, 0] <= -SB_DONE


def _sb_prompt_kernel(q_ref, k_ref, v_ref, o_ref, acc_ref, carry_ref):
    i = pl.program_id(2)
    tq = q_ref.shape[0]
    tk = tq
    acc_ref[...] = jnp.zeros(acc_ref.shape, F32)
    carry_ref[...] = jnp.zeros(carry_ref.shape, F32)
    q2 = q_ref[...] * (ATT_DH ** -0.5)
    lane = _iota((tq, LANE), 1)
    qm = [jnp.where((lane >= ATT_DH) == (hh == 1), q2, 0.0).astype(BF16) for hh in range(2)]
    upper = _upper_ones(tk)

    def tile(j, earlier):
        off = pl.multiple_of(j * tk, tk)
        k2 = k_ref[pl.ds(off, tk), :].astype(BF16)
        v2 = v_ref[pl.ds(off, tk), :].astype(BF16)
        for hh in range(2):
            a, carry = _sb_tile(_dot_nt(qm[hh], k2), earlier, carry_ref[hh], upper)
            acc_ref[hh] += _dot(a.astype(BF16), v2)
            carry_ref[hh] = carry
        return _all_done([carry_ref[0], carry_ref[1]])

    done = tile(i, _iota((tq, tk), 1) < _iota((tq, tk), 0))

    def body(state):
        j, _ = state
        return j - 1, tile(j, None).astype(jnp.int32)

    lax.while_loop(lambda st: (st[0] >= 0) & (st[1] == 0), body, (i - 1, done.astype(jnp.int32)))
    o_ref[...] = jnp.where(lane < ATT_DH, acc_ref[0], acc_ref[1])


def _sb_prompt(proj):
    b, t, _ = proj.shape
    tq = min(ATT_TILE, t)
    return pl.pallas_call(
        _sb_prompt_kernel,
        grid=(b, ATT_HEADS // 2, t // tq),
        in_specs=[
            pl.BlockSpec((None, tq, LANE), lambda i, p, n: (i, n, OFF_SBQ // LANE + p)),
            pl.BlockSpec((None, t, LANE), lambda i, p, n: (i, 0, OFF_SBK // LANE + p)),
            pl.BlockSpec((None, t, LANE), lambda i, p, n: (i, 0, OFF_SBV // LANE + p)),
        ],
        out_specs=pl.BlockSpec((None, tq, LANE), lambda i, p, n: (i, n, p)),
        out_shape=jax.ShapeDtypeStruct((b, t, ATT_HEADS * ATT_DH), F32),
        scratch_shapes=[pltpu.VMEM((2, tq, LANE), F32), pltpu.VMEM((2, tq, 1), F32)],
        compiler_params=_params("parallel", "parallel", "arbitrary"),
    )(proj, proj, proj)


def _sb_sample_kernel(pt_ref, l_ref, q_ref, kn_ref, vn_ref, acc0_ref, carry0_ref, kp_ref, vp_ref, o_ref, co_ref,
                      acc_ref, carry_ref, *, with_new):
    jj = pl.program_id(1)
    nq = q_ref.shape[0] // ATT_HEADS
    upper = _upper_ones(PAGE_SIZE)
    qf = q_ref[...] * (ATT_DH ** -0.5)
    qh = [qf[nq * h:nq * (h + 1)].astype(BF16) for h in range(ATT_HEADS)]

    def visit(kt_ref, vt_ref, earlier):
        z = jnp.concatenate([_dot(qh[h], kt_ref[h].astype(BF16)) for h in range(ATT_HEADS)], axis=0)
        a, carry = _sb_tile(z, earlier, carry_ref[...], upper)
        acc_ref[...] += jnp.concatenate(
            [_dot_nt(a[nq * h:nq * (h + 1)].astype(BF16), vt_ref[h].astype(BF16)) for h in range(ATT_HEADS)], axis=0)
        carry_ref[...] = carry

    @pl.when(jj == 0)
    def _():
        acc_ref[...] = acc0_ref[...]
        carry_ref[...] = carry0_ref[...]
        if with_new:
            rows = _iota((ATT_HEADS * nq, PAGE_SIZE), 0) & (nq - 1)
            visit(kn_ref, vn_ref, _iota((ATT_HEADS * nq, PAGE_SIZE), 1) < rows)

    @pl.when(jnp.logical_not(_all_done(carry_ref[...])))
    def _():
        visit(kp_ref, vp_ref, None)

    @pl.when(jj == pl.num_programs(1) - 1)
    def _():
        o_ref[...] = acc_ref[...]
        co_ref[...] = carry_ref[...]


def _sb_sample_pages(q, kn_t, vn_t, acc0, carry0, cache_k, cache_v, page_table, layer, first_page, n_visit, with_new):
    b, rows, _ = q.shape
    page = lambda i, jj, pt, l: (l[0], pt[i, first_page - jj], 0, 0, 0)
    per_b3 = lambda i, jj, pt, l: (i, 0, 0)
    per_b4 = lambda i, jj, pt, l: (i, 0, 0, 0)
    kv_t = (None, ATT_HEADS, ATT_DH, PAGE_SIZE)
    grid_spec = pltpu.PrefetchScalarGridSpec(
        num_scalar_prefetch=2,
        grid=(b, n_visit),
        in_specs=[
            pl.BlockSpec((None, rows, ATT_DH), per_b3),
            pl.BlockSpec(kv_t, per_b4),
            pl.BlockSpec(kv_t, per_b4),
            pl.BlockSpec((None, rows, ATT_DH), per_b3),
            pl.BlockSpec((None, rows, 1), per_b3),
            pl.BlockSpec((None,) + kv_t, page),
            pl.BlockSpec((None,) + kv_t, page),
        ],
        out_specs=[pl.BlockSpec((None, rows, ATT_DH), per_b3), pl.BlockSpec((None, rows, 1), per_b3)],
        scratch_shapes=[pltpu.VMEM((rows, ATT_DH), F32), pltpu.VMEM((rows, 1), F32)],
    )
    return pl.pallas_call(
        functools.partial(_sb_sample_kernel, with_new=with_new),
        grid_spec=grid_spec,
        out_shape=[jax.ShapeDtypeStruct((b, rows, ATT_DH), F32), jax.ShapeDtypeStruct((b, rows, 1), F32)],
        compiler_params=_params("parallel", "arbitrary"),
    )(page_table, layer, q, kn_t, vn_t, acc0, carry0, cache_k, cache_v)


def _sb_sample(q, kn_t, vn_t, cache_k, cache_v, page_table, layer):
    b, rows, _ = q.shape
    n_pages = page_table.shape[1]
    first = min(SB_FIRST_PAGES, n_pages)
    acc, carry = _sb_sample_pages(q, kn_t, vn_t, jnp.zeros((b, rows, ATT_DH), F32), jnp.zeros((b, rows, 1), F32),
                                  cache_k, cache_v, page_table, layer, n_pages - 1, first, True)
    if first == n_pages:
        return acc
    older = lambda args: _sb_sample_pages(q, kn_t, vn_t, args[0], args[1], cache_k, cache_v, page_table, layer,
                                          n_pages - 1 - first, n_pages - first, False)[0]
    return lax.cond(jnp.max(carry) > -SB_DONE, older, lambda args: args[0], (acc, carry))


def _select_bias(scores, own, axis=1):
    blk = _iota(scores.shape, axis)
    s = jnp.where(blk < own, scores, -jnp.inf)
    keep = blk == own
    for _ in range(MOBA_TOPK):
        m = jnp.max(s, axis=axis, keepdims=True)
        is_max = (s == m) & (m > -jnp.inf)
        first = jnp.min(jnp.where(is_max, blk, LANE), axis=axis, keepdims=True)
        pick = blk == first
        keep = keep | pick
        s = jnp.where(pick, -jnp.inf, s)
    return jnp.where(keep, 0.0, NEG_BIAS)


def _kmean_prompt_kernel(k_ref, o_ref):
    o_ref[...] = jnp.mean(k_ref[...], axis=0, keepdims=True)


def _kmean_prompt(proj):
    b, t, _ = proj.shape
    nb = t // MOBA_BLOCK
    w = ATT_HEADS * ATT_DH
    return pl.pallas_call(
        _kmean_prompt_kernel,
        grid=(b, nb),
        in_specs=[pl.BlockSpec((None, MOBA_BLOCK, w), lambda i, n: (i, n, OFF_MBK // w))],
        out_specs=pl.BlockSpec((None, None, 1, w), lambda i, n: (i, n, 0, 0)),
        out_shape=jax.ShapeDtypeStruct((b, nb, 1, w), F32),
        compiler_params=_params("parallel", "parallel"),
    )(proj)


def _select_prompt_kernel(q_ref, km_ref, o_ref):
    i = pl.program_id(2)
    tq = q_ref.shape[0]
    q = q_ref[...]
    km = km_ref[...]
    lane = _iota((tq, LANE), 1)
    own = (i * tq + _iota((1, tq), 1)) >> 8
    for hh in range(2):
        qm = jnp.where((lane >= ATT_DH) == (hh == 1), q, 0.0)
        scores_t = lax.dot_general(km, qm, NT_DIMS, precision=lax.Precision.HIGHEST, preferred_element_type=F32)
        o_ref[hh] = _select_bias(scores_t, own, axis=0)


def _select_prompt(proj, kmean):
    b, t, _ = proj.shape
    tq = min(ATT_TILE, t)
    return pl.pallas_call(
        _select_prompt_kernel,
        grid=(b, ATT_HEADS // 2, t // tq),
        in_specs=[
            pl.BlockSpec((None, tq, LANE), lambda i, p, n: (i, n, OFF_MBQ // LANE + p)),
            pl.BlockSpec((None, LANE, LANE), lambda i, p, n: (i, 0, p)),
        ],
        out_specs=pl.BlockSpec((None, 2, LANE, tq), lambda i, p, n: (i, p, 0, n)),
        out_shape=jax.ShapeDtypeStruct((b, ATT_HEADS, LANE, t), F32),
        compiler_params=_params("parallel", "parallel", "parallel"),
    )(proj, kmean)


def _moba_prompt_kernel(q_ref, k_ref, vt_ref, bias_ref, o_ref, acc_ref, m_ref, sa_ref, sb_ref):
    i = pl.program_id(2)
    tq = q_ref.shape[0]
    tk = tq
    acc_ref[...] = jnp.zeros(acc_ref.shape, F32)
    m_ref[...] = jnp.full(m_ref.shape, -jnp.inf, F32)
    q2 = q_ref[...] * (ATT_DH ** -0.5)
    lane = _iota((tq, LANE), 1)
    qm = [jnp.where((lane >= ATT_DH) == (hh == 1), q2, 0.0).astype(BF16) for hh in range(2)]
    first_rows = _iota((LANE, tk), 0) < ATT_DH
    head_rows = [first_rows, jnp.logical_not(first_rows)]

    key_row = _iota((tk, tq), 0)
    query = i * tq + _iota((tk, tq), 1)

    def scores(j, dst_ref):
        jc = jnp.maximum(j, 0)
        k2 = k_ref[pl.ds(pl.multiple_of(jc * tk, tk), tk), :].astype(BF16)
        for hh in range(2):
            dst_ref[hh] = _dot_nt(k2, qm[hh]) + bias_ref[hh, pl.ds(jc, 1), :]

    def accumulate(j, src_ref):
        vt = vt_ref[:, pl.ds(pl.multiple_of(jnp.maximum(j, 0) * tk, tk), tk)]
        key = j * tk + key_row
        visible = (key <= query) & (key >= 0)
        for hh in range(2):
            vt_h = jnp.where(head_rows[hh], vt, 1.0).astype(BF16)
            s = jnp.where(visible, src_ref[hh], -jnp.inf)
            m_prev = m_ref[hh]
            m_new = jnp.maximum(m_prev, jnp.max(s, axis=0, keepdims=True))
            p = jnp.exp(s - m_new)
            acc_ref[hh] = jnp.exp(m_prev - m_new) * acc_ref[hh] + _dot(vt_h, p.astype(BF16))
            m_ref[hh] = m_new

    scores(i, sa_ref)

    def body(u, c):
        j = i - 2 * u
        accumulate(j, sa_ref)
        scores(j - 1, sb_ref)
        accumulate(j - 1, sb_ref)
        scores(j - 2, sa_ref)
        return c

    lax.fori_loop(0, (i + 2) // 2, body, 0)
    a0, a1 = acc_ref[0], acc_ref[1]
    o_t = jnp.concatenate([a0[:ATT_DH] / a0[ATT_DH:], a1[ATT_DH:] / a1[:ATT_DH]], axis=0)
    o_ref[...] = o_t.T


def _moba_prompt(proj, v_t, bias):
    b, t, _ = proj.shape
    tq = min(ATT_TILE, t)
    assert tq == MOBA_BLOCK, "the prompt MoBA kernel visits one key block per tile"
    return pl.pallas_call(
        _moba_prompt_kernel,
        grid=(b, ATT_HEADS // 2, t // tq),
        in_specs=[
            pl.BlockSpec((None, tq, LANE), lambda i, p, n: (i, n, OFF_MBQ // LANE + p)),
            pl.BlockSpec((None, t, LANE), lambda i, p, n: (i, 0, OFF_MBK // LANE + p)),
            pl.BlockSpec((None, LANE, t), lambda i, p, n: (i, p, 0)),
            pl.BlockSpec((None, 2, LANE, tq), lambda i, p, n: (i, p, 0, n)),
        ],
        out_specs=pl.BlockSpec((None, tq, LANE), lambda i, p, n: (i, n, p)),
        out_shape=jax.ShapeDtypeStruct((b, t, ATT_HEADS * ATT_DH), F32),
        scratch_shapes=[pltpu.VMEM((2, LANE, tq), F32), pltpu.VMEM((2, 1, tq), F32),
                        pltpu.VMEM((2, tq, tq), F32), pltpu.VMEM((2, tq, tq), F32)],
        compiler_params=_params("parallel", "parallel", "arbitrary"),
    )(proj, proj, v_t, bias)


def _put_lane(ref, n, col):
    ref[...] = jnp.where(_iota(ref.shape, 1) == n, col, ref[...])


def _moba_sample_kernel(pt_ref, l_ref, q_ref, kn_ref, vn_ref, k0_ref, k1_ref, v0_ref, v1_ref, o_ref,
                        score_ref, m_ref, l_blk_ref, o_blk_ref, own_m_ref, own_l_ref, own_o_ref, *, own):
    n = pl.program_id(1)
    nq = q_ref.shape[0] // ATT_HEADS
    rows = ATT_HEADS * nq
    qf = q_ref[...]
    qh = [qf[nq * h:nq * (h + 1)].astype(BF16) for h in range(ATT_HEADS)]
    scale = ATT_DH ** -0.5

    def logits(kt_refs):
        return jnp.concatenate(
            [jnp.concatenate([_dot(qh[h], kt[h].astype(BF16)) for kt in kt_refs], axis=1)
             for h in range(ATT_HEADS)], axis=0)

    def partials(s, vt_refs):
        m = jnp.max(s, axis=1, keepdims=True)
        p = jnp.exp(s - m)
        o = None
        for c, vt in enumerate(vt_refs):
            oc = jnp.concatenate(
                [_dot_nt(p[nq * h:nq * (h + 1), PAGE_SIZE * c:PAGE_SIZE * (c + 1)].astype(BF16), vt[h].astype(BF16))
                 for h in range(ATT_HEADS)], axis=0)
            o = oc if o is None else o + oc
        return m, jnp.sum(p, axis=1, keepdims=True), o

    @pl.when(n == 0)
    def _():
        score_ref[...] = jnp.zeros(score_ref.shape, F32)
        m_ref[...] = jnp.zeros(m_ref.shape, F32)
        l_blk_ref[...] = jnp.zeros(l_blk_ref.shape, F32)
        qrow = _iota((rows, PAGE_SIZE), 0) & (nq - 1)
        s = jnp.where(_iota((rows, PAGE_SIZE), 1) <= qrow, logits([kn_ref]) * scale, -jnp.inf)
        own_m_ref[...], own_l_ref[...], own_o_ref[...] = partials(s, [vn_ref])

    z = logits([k0_ref, k1_ref])
    _put_lane(score_ref, n, jnp.sum(z, axis=1, keepdims=True) * (1.0 / MOBA_BLOCK))
    m, l, o = partials(z * scale, [v0_ref, v1_ref])
    _put_lane(m_ref, n, m)
    _put_lane(l_blk_ref, n, l)
    o_blk_ref[n] = o

    @pl.when(n == pl.num_programs(1) - 1)
    def _():
        keep = _select_bias(score_ref[...], jnp.full((rows, 1), own, jnp.int32)) == 0.0
        keep = keep & (_iota((rows, LANE), 1) < own)
        m_own, l_own = own_m_ref[...], own_l_ref[...]
        m_blk = m_ref[...]
        m_all = jnp.maximum(m_own, jnp.max(jnp.where(keep, m_blk, -jnp.inf), axis=1, keepdims=True))
        w = jnp.where(keep, jnp.exp(m_blk - m_all), 0.0)
        w_own = jnp.exp(m_own - m_all)
        l_all = w_own * l_own + jnp.sum(w * l_blk_ref[...], axis=1, keepdims=True)
        acc = w_own * own_o_ref[...]
        for blk in range(own):
            acc = acc + w[:, blk:blk + 1] * o_blk_ref[blk]
        o_ref[...] = acc / l_all


def _moba_sample(q, kn_t, vn_t, cache_k, cache_v, page_table, layer):
    b, rows, _ = q.shape
    n_pages = page_table.shape[1]
    ppb = MOBA_BLOCK // PAGE_SIZE
    assert ppb == 2 and n_pages % ppb == 0 and n_pages // ppb < LANE
    n_blocks = n_pages // ppb
    page = lambda c: (lambda i, n, pt, l: (l[0], pt[i, ppb * n + c], 0, 0, 0))
    per_b3 = lambda i, n, pt, l: (i, 0, 0)
    per_b4 = lambda i, n, pt, l: (i, 0, 0, 0)
    kv_t = (None, ATT_HEADS, ATT_DH, PAGE_SIZE)
    grid_spec = pltpu.PrefetchScalarGridSpec(
        num_scalar_prefetch=2,
        grid=(b, n_blocks),
        in_specs=[
            pl.BlockSpec((None, rows, ATT_DH), per_b3),
            pl.BlockSpec(kv_t, per_b4),
            pl.BlockSpec(kv_t, per_b4),
            pl.BlockSpec((None,) + kv_t, page(0)),
            pl.BlockSpec((None,) + kv_t, page(1)),
            pl.BlockSpec((None,) + kv_t, page(0)),
            pl.BlockSpec((None,) + kv_t, page(1)),
        ],
        out_specs=pl.BlockSpec((None, rows, ATT_DH), per_b3),
        scratch_shapes=[pltpu.VMEM((rows, LANE), F32), pltpu.VMEM((rows, LANE), F32), pltpu.VMEM((rows, LANE), F32),
                        pltpu.VMEM((n_blocks, rows, ATT_DH), F32), pltpu.VMEM((rows, 1), F32),
                        pltpu.VMEM((rows, 1), F32), pltpu.VMEM((rows, ATT_DH), F32)],
    )
    return pl.pallas_call(
        functools.partial(_moba_sample_kernel, own=n_blocks),
        grid_spec=grid_spec,
        out_shape=jax.ShapeDtypeStruct((b, rows, ATT_DH), F32),
        compiler_params=_params("parallel", "arbitrary"),
    )(page_table, layer, q, kn_t, vn_t, cache_k, cache_k, cache_v, cache_v)


def _merge_kernel(o0_ref, o1_ref, o2_ref, o3_ref, g0_ref, g1_ref, g2_ref, g3_ref, x_ref, wb_ref, wo_ref, nw_ref,
                  x1_ref, hf_ref):
    mixed = None
    for o_ref, g_ref, i in ((o0_ref, g0_ref, 0), (o1_ref, g1_ref, 1), (o2_ref, g2_ref, 2), (o3_ref, g3_ref, 3)):
        term = _sigmoid(g_ref[...]) * _dot(o_ref[...].astype(BF16), wb_ref[i])
        mixed = term if mixed is None else mixed + term
    x1 = x_ref[...] + _dot(mixed.astype(BF16), wo_ref[...])
    x1_ref[...] = x1
    ms = jnp.mean(x1 * x1, axis=-1, keepdims=True)
    hf_ref[...] = (x1 * lax.rsqrt(ms + EPS) * nw_ref[...]).astype(BF16)


def _merge(branches, proj, x, lp):
    n, d = x.shape
    tm = _tile(n, 256)
    row = lambda i: (i, 0)
    gate = lambda g: (lambda i: (i, OFF_GATE // d + g))
    return pl.pallas_call(
        _merge_kernel,
        grid=(n // tm,),
        in_specs=[pl.BlockSpec((tm, BRANCH_WIDTH), row)] * 4
        + [pl.BlockSpec((tm, d), gate(g)) for g in range(4)]
        + [
            pl.BlockSpec((tm, d), row),
            pl.BlockSpec((4, BRANCH_WIDTH, d), lambda i: (0, 0, 0)),
            pl.BlockSpec((d, d), lambda i: (0, 0)),
            pl.BlockSpec((1, d), lambda i: (0, 0)),
        ],
        out_specs=[pl.BlockSpec((tm, d), row), pl.BlockSpec((tm, d), row)],
        out_shape=[jax.ShapeDtypeStruct((n, d), F32), jax.ShapeDtypeStruct((n, d), BF16)],
        compiler_params=_params("parallel"),
    )(*branches, proj, proj, proj, proj, x, lp["w_branch"], lp["w_out"], lp["norm_ffn"])


def _ffn_kernel(u_ref, halo_ref, halo0_ref, x1_ref, cw_ref, cb_ref, wd_ref, o_ref, ext_ref, *, tiles_per_seq):
    i = pl.program_id(0)
    tm = u_ref.shape[0]
    seq_start = (i % tiles_per_seq) == 0

    @pl.when(seq_start)
    def _():
        ext_ref[0:SUBLANE, :] = halo0_ref[...]

    @pl.when(jnp.logical_not(seq_start))
    def _():
        ext_ref[0:SUBLANE, :] = halo_ref[...]

    u = u_ref[...]
    ext_ref[SUBLANE:SUBLANE + tm, :] = u
    cw = cw_ref[...]
    conv = u * cw[FFN_CONV - 1:FFN_CONV] + cb_ref[...]
    for t in range(FFN_CONV - 1):
        off = SUBLANE - (FFN_CONV - 1) + t
        conv = conv + ext_ref[off:off + tm, :] * cw[t:t + 1]
    act = _silu(conv[:, :D_FF]) * conv[:, D_FF:]
    o_ref[...] = x1_ref[...] + _dot(act.astype(BF16), wd_ref[...])


def _ffn(u, halo0, x1, seq_len, lp):
    n, w = u.shape
    d = x1.shape[1]
    tm = _tile(seq_len, 256)
    tiles_per_seq = seq_len // tm
    hb = tm // SUBLANE
    return pl.pallas_call(
        functools.partial(_ffn_kernel, tiles_per_seq=tiles_per_seq),
        grid=(n // tm,),
        in_specs=[
            pl.BlockSpec((tm, w), lambda i: (i, 0)),
            pl.BlockSpec((SUBLANE, w), lambda i: (jnp.maximum(i * hb - 1, 0), 0)),
            pl.BlockSpec((None, SUBLANE, w), lambda i: (i // tiles_per_seq, 0, 0)),
            pl.BlockSpec((tm, d), lambda i: (i, 0)),
            pl.BlockSpec((FFN_CONV, w), lambda i: (0, 0)),
            pl.BlockSpec((1, w), lambda i: (0, 0)),
            pl.BlockSpec((D_FF, d), lambda i: (0, 0)),
        ],
        out_specs=pl.BlockSpec((tm, d), lambda i: (i, 0)),
        out_shape=jax.ShapeDtypeStruct((n, d), F32),
        scratch_shapes=[pltpu.VMEM((tm + SUBLANE, w), F32)],
        compiler_params=_params("parallel"),
    )(u, u, halo0, x1, lp["ffn_conv_w"], lp["ffn_conv_b"], lp["w_down"])


def _rope_tables(pos0, t):
    half = RET_DK // 2
    inv = ROPE_BASE ** (-jnp.arange(half, dtype=F32) / half)
    ang = (pos0 + jnp.arange(t, dtype=jnp.int32)).astype(F32)[:, None] * inv[None, :]
    cos, sin = jnp.cos(ang), jnp.sin(ang)
    cos_t = jnp.tile(jnp.concatenate([cos, cos], axis=1), (1, RET_HEADS))
    sin_t = jnp.tile(jnp.concatenate([-sin, sin], axis=1), (1, RET_HEADS))
    return cos_t, sin_t


def _pad_state_rows(s):
    return jnp.pad(s, ((0, 0), (SUBLANE - s.shape[1], 0), (0, 0)))


def _ssm_state_in(h0):
    b = h0.shape[0]
    h2 = h0.reshape(b, SSM_INNER, SSM_STATE)
    g0 = (jnp.arange(SSM_INNER) < SSM_INNER // SSM_GROUPS)[None, :, None]
    return jnp.concatenate([jnp.where(g0, h2, 0.0), jnp.where(g0, 0.0, h2)], axis=-1)


def _ssm_state_out(h2):
    b = h2.shape[0]
    g0 = (jnp.arange(SSM_INNER) < SSM_INNER // SSM_GROUPS)[None, :, None]
    return jnp.where(g0, h2[..., :SSM_STATE], h2[..., SSM_STATE:]).reshape(b, SSM_HEADS, SSM_HEADDIM, SSM_STATE)


def _prep_layer_weights(w):
    depth = w["w_in"].shape[0]
    w_in = w["w_in"]
    e_xbc = OFF_Z + SSM_INNER + SSM_CONV_DIM
    w_in = jnp.concatenate([
        w_in[..., :OFF_Z + SSM_INNER],
        w_in[..., e_xbc:e_xbc + SSM_HEADS],
        jnp.zeros(w_in.shape[:2] + (DT_PAD - SSM_HEADS,), w_in.dtype),
        w_in[..., OFF_Z + SSM_INNER:e_xbc],
        w_in[..., e_xbc + SSM_HEADS:],
    ], axis=-1)
    assert w_in.shape[-1] == P_TOTAL
    pad_heads = lambda a: jnp.pad(a, ((0, 0), (0, LANE - SSM_HEADS)))[:, None, :]
    return {
        "norm_mix": w["norm_mix"],
        "w_in": w_in.astype(BF16),
        "ssm_conv_w": w["ssm_conv_w"],
        "ssm_conv_b": w["ssm_conv_b"][:, None, :],
        "ssm_dt_bias": pad_heads(w["ssm_dt_bias"]),
        "ssm_a_log": pad_heads(w["ssm_a_log"]),
        "ssm_d": jnp.repeat(w["ssm_d"], SSM_HEADDIM, axis=1)[:, None, :],
        "ssm_norm": w["ssm_norm"][:, None, :],
        "w_branch": w["w_branch"].astype(BF16),
        "w_out": w["w_out"].astype(BF16),
        "norm_ffn": w["norm_ffn"][:, None, :],
        "w_up": w["w_up"].astype(BF16),
        "ffn_conv_w": w["ffn_conv_w"],
        "ffn_conv_b": w["ffn_conv_b"][:, None, :],
        "w_down": w["w_down"].astype(BF16),
        "layer": jnp.arange(depth, dtype=jnp.int32)[:, None],
    }


def _heads_major(a, b, t):
    return a.reshape(b, t, ATT_HEADS, ATT_DH).transpose(0, 2, 1, 3)


def _layer(x, lp, past, pos0, caches, page_table):
    b, t, d = x.shape
    n = b * t
    r0, h0, conv0, ffn0 = past
    x2d = x.reshape(n, d)
    hn = _rmsnorm(x2d, lp["norm_mix"], BF16)
    proj = _matmul(hn, lp["w_in"])
    proj3 = proj.reshape(b, t, P_TOTAL)
    w_att = ATT_HEADS * ATT_DH

    cos, sin = _rope_tables(pos0, t)
    o_ret, r_new = _retention(proj3, cos, sin, r0)
    o_ssm, conv_new, h_new = _ssd(proj3, _pad_state_rows(conv0), _ssm_state_in(h0), lp)

    sbk, sbv = proj[:, OFF_SBK:OFF_SBK + w_att], proj[:, OFF_SBV:OFF_SBV + w_att]
    mbk, mbv = proj[:, OFF_MBK:OFF_MBK + w_att], proj[:, OFF_MBV:OFF_MBV + w_att]
    if caches is None:
        o_sb = _sb_prompt(proj3)
        kmean = _kmean_prompt(proj3)[:, :, 0, :]
        kmean = jnp.pad(kmean, ((0, 0), (0, LANE - kmean.shape[1]), (0, 0)))
        v_t = proj3[:, :, OFF_MBV:OFF_MBV + w_att].transpose(0, 2, 1)
        o_mb = _moba_prompt(proj3, v_t, _select_prompt(proj3, kmean))
    else:
        csk, csv, cmk, cmv = caches
        layer = lp["layer"]
        rows = ATT_HEADS * t
        new_t = lambda a: jnp.pad(a.reshape(b, t, ATT_HEADS, ATT_DH).transpose(0, 2, 3, 1),
                                  ((0, 0), (0, 0), (0, 0), (0, PAGE_SIZE - t)))
        to_rows = lambda a: _heads_major(a, b, t).reshape(b, rows, ATT_DH)
        from_rows = lambda o: o.reshape(b, ATT_HEADS, t, ATT_DH).transpose(0, 2, 1, 3).reshape(b, t, w_att)
        sq = to_rows(proj[:, OFF_SBQ:OFF_SBQ + w_att])
        o_sb = from_rows(_sb_sample(sq, new_t(sbk), new_t(sbv), csk, csv, page_table, layer))
        mq = to_rows(proj[:, OFF_MBQ:OFF_MBQ + w_att])
        o_mb = from_rows(_moba_sample(mq, new_t(mbk), new_t(mbv), cmk, cmv, page_table, layer))

    branches = [o.reshape(n, BRANCH_WIDTH) for o in (o_ret, o_sb, o_mb, o_ssm)]
    x1, hf = _merge(branches, proj, x2d, lp)
    u = _matmul(hf, lp["w_up"])
    x2 = _ffn(u, _pad_state_rows(ffn0), x1, t, lp)
    heads = lambda a: a.reshape(b, t, ATT_HEADS, ATT_DH)
    state = (heads(sbk), heads(sbv), heads(mbk), heads(mbv), r_new, _ssm_state_out(h_new),
             conv_new[:, SUBLANE - (SSM_CONV - 1):], u.reshape(b, t, -1)[:, t - (FFN_CONV - 1):])
    return x2.reshape(b, t, d), state


def _trunk(x, pos0, past, lw, norm_final, caches, page_table):
    def body(carry, per_layer):
        lp, pst = per_layer
        y, st = _layer(carry, lp, pst, pos0, caches, page_table)
        return y, st

    y, states = lax.scan(body, x, (lw, past))
    b, t, d = y.shape
    out = _rmsnorm(y.reshape(b * t, d), norm_final, F32).reshape(b, t, d)
    return out, states


def kernel(x_prompt, x_sample, cache_sb_k, cache_sb_v, cache_moba_k, cache_moba_v, page_table, state_ret, state_ssm,
           state_ssm_conv, state_ffn_conv, norm_mix, w_in, ssm_conv_w, ssm_conv_b, ssm_dt_bias, ssm_a_log, ssm_d,
           ssm_norm, w_branch, w_out, norm_ffn, w_up, ffn_conv_w, ffn_conv_b, w_down, norm_final):
    lw = _prep_layer_weights({
        "norm_mix": norm_mix, "w_in": w_in, "ssm_conv_w": ssm_conv_w, "ssm_conv_b": ssm_conv_b,
        "ssm_dt_bias": ssm_dt_bias, "ssm_a_log": ssm_a_log, "ssm_d": ssm_d, "ssm_norm": ssm_norm,
        "w_branch": w_branch, "w_out": w_out, "norm_ffn": norm_ffn, "w_up": w_up, "ffn_conv_w": ffn_conv_w,
        "ffn_conv_b": ffn_conv_b, "w_down": w_down})
    depth = w_in.shape[0]
    bp = x_prompt.shape[0]
    zeros = lambda *s: jnp.zeros((depth, bp) + s, F32)
    past_p = (zeros(RET_HEADS, RET_DK, RET_DV), zeros(SSM_HEADS, SSM_HEADDIM, SSM_STATE),
              zeros(SSM_CONV - 1, SSM_CONV_DIM), zeros(FFN_CONV - 1, 2 * D_FF))
    y_p, st_p = _trunk(x_prompt, 0, past_p, lw, norm_final, None, None)

    caches = tuple(c.transpose(0, 1, 3, 4, 2) for c in (cache_sb_k, cache_sb_v, cache_moba_k, cache_moba_v))
    past_s = (state_ret, state_ssm, state_ssm_conv, state_ffn_conv)
    pos0 = page_table.shape[1] * PAGE_SIZE
    y_s, st_s = _trunk(x_sample, pos0, past_s, lw, norm_final, caches, page_table)
    return (y_p, y_s) + tuple(st_p) + tuple(st_s)
```

```python
import functools
import math

import jax
import jax.numpy as jnp
from jax import lax
from jax.experimental import pallas as pl
from jax.experimental.pallas import tpu as pltpu

F32 = jnp.float32
BF16 = jnp.bfloat16

EPS = 1e-6
D_MODEL = 1024
BRANCH_WIDTH = 512
RET_HEADS, RET_DK, RET_DV = 4, 64, 128
ROPE_BASE = 10000.0
ATT_HEADS, ATT_DH = 8, 64
MOBA_BLOCK, MOBA_TOPK = 256, 3
SSM_HEADS, SSM_HEADDIM, SSM_STATE, SSM_GROUPS, SSM_CONV = 8, 64, 64, 2, 4
SSM_INNER = SSM_HEADS * SSM_HEADDIM
SSM_CONV_DIM = SSM_INNER + 2 * SSM_GROUPS * SSM_STATE
D_FF = 2816
FFN_CONV = 3
PAGE_SIZE = 128
CHUNK = 128
ATT_TILE = 256
LOG_GAMMA = tuple(math.log1p(-2.0 ** (-5.0 - h)) for h in range(RET_HEADS))

OFF_RQ, OFF_RK, OFF_RV, OFF_RG = 0, 256, 512, 1024
OFF_SBQ, OFF_SBK, OFF_SBV = 1536, 2048, 2560
OFF_MBQ, OFF_MBK, OFF_MBV = 3072, 3584, 4096
OFF_Z, OFF_DT, OFF_XBC, OFF_GATE = 4608, 5120, 5376, 6144
DT_PAD = 256
P_TOTAL = OFF_GATE + 4 * D_MODEL
NEG_BIAS = -1e30
SB_DONE = 104.0
SB_FIRST_PAGES = 4
MOBA_SAMPLE_BLOCKS_PER_STEP = 2

LANE = 128
SUBLANE = 8
VMEM_LIMIT = 56 * 1024 * 1024

NT_DIMS = (((1,), (1,)), ((), ()))
TN_DIMS = (((0,), (0,)), ((), ()))


def _dot(a, b):
    return jnp.dot(a, b, preferred_element_type=F32)


def _dot_nt(a, b):
    return lax.dot_general(a, b, NT_DIMS, preferred_element_type=F32)


def _dot_tn(a, b):
    return lax.dot_general(a, b, TN_DIMS, preferred_element_type=F32)


def _split_bf16(x, n):
    terms, r = [], x
    for _ in range(n):
        t = r.astype(BF16)
        terms.append(t)
        r = r - t.astype(F32)
    return terms


def _dot_split_rhs(a01, x, n=3):
    out = None
    for t in _split_bf16(x, n):
        d = _dot(a01, t)
        out = d if out is None else out + d
    return out


def _dot_split_lhs(x, a01, n=3):
    out = None
    for t in _split_bf16(x, n):
        d = _dot(t, a01)
        out = d if out is None else out + d
    return out


def _iota(shape, dim):
    return lax.broadcasted_iota(jnp.int32, shape, dim)


def _sigmoid(x):
    return 1.0 / (1.0 + jnp.exp(-x))


def _silu(x):
    return x * _sigmoid(x)


def _log_sigmoid(x):
    return jnp.minimum(x, 0.0) - jnp.log1p(jnp.exp(-jnp.abs(x)))


def _softplus(x):
    return jnp.maximum(x, 0.0) + jnp.log1p(jnp.exp(-jnp.abs(x)))


def _pad_rows(x, rows):
    if x.shape[0] == rows:
        return x
    return jnp.concatenate([x, jnp.zeros((rows - x.shape[0],) + x.shape[1:], x.dtype)], axis=0)


def _tile(n, cap, mult=SUBLANE):
    if n <= cap:
        return n
    for t in range(cap - cap % mult, 0, -mult):
        if n % t == 0:
            return t
    raise ValueError(f"no tile for {n}")


def _params(*sem):
    return pltpu.CompilerParams(dimension_semantics=sem, vmem_limit_bytes=VMEM_LIMIT)


def _rmsnorm_kernel(x_ref, w_ref, o_ref):
    x = x_ref[...]
    ms = jnp.mean(x * x, axis=-1, keepdims=True)
    o_ref[...] = (x * lax.rsqrt(ms + EPS) * w_ref[...]).astype(o_ref.dtype)


def _rmsnorm(x, w, out_dtype):
    n, d = x.shape
    tm = _tile(n, 512)
    return pl.pallas_call(
        _rmsnorm_kernel,
        grid=(n // tm,),
        in_specs=[pl.BlockSpec((tm, d), lambda i: (i, 0)), pl.BlockSpec((1, d), lambda i: (0, 0))],
        out_specs=pl.BlockSpec((tm, d), lambda i: (i, 0)),
        out_shape=jax.ShapeDtypeStruct((n, d), out_dtype),
        compiler_params=_params("parallel"),
    )(x, w.reshape(1, d))


def _matmul_kernel(x_ref, w_ref, o_ref):
    o_ref[...] = _dot(x_ref[...], w_ref[...])


def _matmul(x, w):
    n, k = x.shape
    m = w.shape[1]
    tm = _tile(n, 1024)
    tn = _tile(m, 1536, LANE)
    return pl.pallas_call(
        _matmul_kernel,
        grid=(m // tn, n // tm),
        in_specs=[pl.BlockSpec((tm, k), lambda c, r: (r, 0)), pl.BlockSpec((k, tn), lambda c, r: (0, c))],
        out_specs=pl.BlockSpec((tm, tn), lambda c, r: (r, c)),
        out_shape=jax.ShapeDtypeStruct((n, m), F32),
        compiler_params=_params("parallel", "parallel"),
    )(x, w)


def _retention_kernel(q_ref, k_ref, v_ref, g_ref, cos_ref, sin_ref, r0_ref, o_ref, rn_ref, r_ref, *, c, tv):
    n = pl.program_id(1)
    hk, hv = RET_HEADS * RET_DK, RET_HEADS * RET_DV

    @pl.when(n == 0)
    def _():
        r_ref[...] = jnp.zeros((hk, hv), F32)
        for h in range(RET_HEADS):
            r_ref[RET_DK * h:RET_DK * (h + 1), RET_DV * h:RET_DV * (h + 1)] = r0_ref[h]

    rows = q_ref.shape[0]
    q, k, v = _pad_rows(q_ref[...], c), _pad_rows(k_ref[...], c), _pad_rows(v_ref[...], c)
    cos, sin = _pad_rows(cos_ref[...], c), _pad_rows(sin_ref[...], c)
    lane = _iota((c, hk), 1)
    head_of_lane = lane >> 6
    first_half = (_iota((c, LANE), 1) & 63) < 32

    def rope(x):
        parts = []
        for s in range(hk // LANE):
            xs = x[:, LANE * s:LANE * (s + 1)]
            parts.append(jnp.where(first_half, pltpu.roll(xs, LANE - 32, 1), pltpu.roll(xs, 32, 1)))
        return x * cos + jnp.concatenate(parts, axis=1) * sin

    qr = rope(q) * (RET_DK ** -0.5)
    kr = rope(k)
    lg = jnp.full((c, hk), LOG_GAMMA[RET_HEADS - 1], F32)
    for h in range(RET_HEADS - 2, -1, -1):
        lg = jnp.where(head_of_lane == h, LOG_GAMMA[h], lg)
    ri = _iota((c, hk), 0).astype(F32)
    q_from_start = jnp.exp(lg * (ri + 1.0))
    k_to_end = jnp.exp(lg * (float(tv - 1) - ri))
    kb, vb = kr.astype(BF16), v.astype(BF16)
    di = _iota((c, c), 0) - _iota((c, c), 1)
    dif = jnp.maximum(di, 0).astype(F32)
    inner = []
    for h in range(RET_HEADS):
        decay = jnp.where(di >= 0, jnp.exp(LOG_GAMMA[h] * dif), 0.0)
        qm = jnp.where(head_of_lane == h, qr, 0.0).astype(BF16)
        s = _dot_nt(qm, kb)
        inner.append(_dot((s * decay).astype(BF16), vb[:, RET_DV * h:RET_DV * (h + 1)]))
    r = r_ref[...]
    cross = _dot((qr * q_from_start).astype(BF16), r.astype(BF16))
    ret = jnp.concatenate(inner, axis=1) + cross
    kv = _dot_tn((kr * k_to_end).astype(BF16), vb)
    row_head = _iota((hk, hv), 0) >> 6
    col_head = _iota((hk, hv), 1) >> 7
    g_chunk = jnp.full((hk, hv), math.exp(LOG_GAMMA[RET_HEADS - 1] * tv), F32)
    for h in range(RET_HEADS - 2, -1, -1):
        g_chunk = jnp.where(row_head == h, math.exp(LOG_GAMMA[h] * tv), g_chunk)
    r_new = g_chunk * r + jnp.where(row_head == col_head, kv, 0.0)
    r_ref[...] = r_new
    g = _pad_rows(g_ref[...], c)
    outs = []
    for h in range(RET_HEADS):
        x = ret[:, RET_DV * h:RET_DV * (h + 1)]
        y = x * lax.rsqrt(jnp.mean(x * x, axis=-1, keepdims=True) + EPS)
        outs.append(y * _silu(g[:, RET_DV * h:RET_DV * (h + 1)]))
    o_ref[...] = jnp.concatenate(outs, axis=1)[:rows]

    @pl.when(n == pl.num_programs(1) - 1)
    def _():
        for h in range(RET_HEADS):
            rn_ref[h] = r_new[RET_DK * h:RET_DK * (h + 1), RET_DV * h:RET_DV * (h + 1)]


def _retention(proj, cos, sin, r0):
    b, t, _ = proj.shape
    cb = min(t, CHUNK)
    hk, hv = RET_HEADS * RET_DK, RET_HEADS * RET_DV
    kern = functools.partial(_retention_kernel, c=CHUNK, tv=cb)
    return pl.pallas_call(
        kern,
        grid=(b, t // cb),
        in_specs=[
            pl.BlockSpec((None, cb, hk), lambda i, n: (i, n, OFF_RQ // hk)),
            pl.BlockSpec((None, cb, hk), lambda i, n: (i, n, OFF_RK // hk)),
            pl.BlockSpec((None, cb, hv), lambda i, n: (i, n, OFF_RV // hv)),
            pl.BlockSpec((None, cb, hv), lambda i, n: (i, n, OFF_RG // hv)),
            pl.BlockSpec((cb, hk), lambda i, n: (n, 0)),
            pl.BlockSpec((cb, hk), lambda i, n: (n, 0)),
            pl.BlockSpec((None, RET_HEADS, RET_DK, RET_DV), lambda i, n: (i, 0, 0, 0)),
        ],
        out_specs=[
            pl.BlockSpec((None, cb, hv), lambda i, n: (i, n, 0)),
            pl.BlockSpec((None, RET_HEADS, RET_DK, RET_DV), lambda i, n: (i, 0, 0, 0)),
        ],
        out_shape=[jax.ShapeDtypeStruct((b, t, hv), F32), jax.ShapeDtypeStruct(r0.shape, F32)],
        scratch_shapes=[pltpu.VMEM((hk, hv), F32)],
        compiler_params=_params("parallel", "arbitrary"),
    )(proj, proj, proj, proj, cos, sin, r0)


def _ssd_kernel(z_ref, dt_ref, xbc_ref, conv0_ref, h0_ref, cw_ref, cb_ref, dtb_ref, alog_ref, dx_ref, nw_ref,
                o_ref, convn_ref, hn_ref, xp_ref, h_ref, *, c, tv):
    n = pl.program_id(1)
    rows = xbc_ref.shape[0]
    half = SSM_INNER // SSM_GROUPS

    @pl.when(n == 0)
    def _():
        xp_ref[0:SUBLANE, :] = conv0_ref[...]
        h_ref[...] = h0_ref[...]

    xraw = _pad_rows(xbc_ref[...], c)
    xp_ref[SUBLANE:SUBLANE + c, :] = xraw
    cw = cw_ref[...]
    acc = xraw * cw[SSM_CONV - 1:SSM_CONV] + cb_ref[...]
    for i in range(SSM_CONV - 1):
        off = SUBLANE - (SSM_CONV - 1) + i
        acc = acc + xp_ref[off:off + c, :] * cw[i:i + 1]
    tail = xp_ref[tv:tv + SUBLANE, :]
    xp_ref[0:SUBLANE, :] = tail
    xbc = _silu(acc)
    xs = xbc[:, :SSM_INNER]
    bm = xbc[:, SSM_INNER:SSM_INNER + LANE]
    cm = xbc[:, SSM_INNER + LANE:SSM_INNER + 2 * LANE]

    dt = _softplus(_pad_rows(dt_ref[...], c) + dtb_ref[...])
    if tv < c:
        dt = jnp.where(_iota((c, LANE), 0) < tv, dt, 0.0)
    da = dt * (-jnp.exp(alog_ref[...]))
    tril = _iota((c, c), 0) >= _iota((c, c), 1)
    cum = _dot_split_rhs(tril.astype(BF16), da)
    cum_t = cum.T
    expand = ((_iota((LANE, SSM_INNER), 1) >> 6) == _iota((LANE, SSM_INNER), 0)).astype(BF16)
    cumx = _dot_split_lhs(cum, expand)
    dtx = _dot_split_lhs(dt, expand)
    to_end = jnp.exp(cumx[tv - 1:tv, :] - cumx) * dtx
    x_to_end = (xs * to_end).astype(BF16)
    x_dt = (xs * dtx).astype(BF16)
    bb = bm.astype(BF16)
    lane = _iota((c, LANE), 1)
    cg = [jnp.where((lane >> 6) == g, cm, 0.0).astype(BF16) for g in range(SSM_GROUPS)]
    scores = [_dot_nt(cg[g], bb) for g in range(SSM_GROUPS)]
    y_pairs = []
    for p in range(SSM_HEADS // 2):
        y_head = []
        for hh in range(2):
            h = 2 * p + hh
            seg = cum[:, h:h + 1] - cum_t[h:h + 1, :]
            m = scores[h // (SSM_HEADS // SSM_GROUPS)] * jnp.exp(jnp.where(tril, seg, -jnp.inf))
            y_head.append(_dot(m.astype(BF16), x_dt[:, LANE * p:LANE * (p + 1)]))
        y_pairs.append(jnp.where(lane < SSM_HEADDIM, y_head[0], y_head[1]))
    hs = h_ref[...]
    hsb = hs.astype(BF16)
    y_off = jnp.concatenate([_dot_nt(cg[g], hsb[half * g:half * (g + 1)]) for g in range(SSM_GROUPS)], axis=1)
    y = jnp.concatenate(y_pairs, axis=1) + y_off * jnp.exp(cumx) + dx_ref[...] * xs
    tz = y * _silu(_pad_rows(z_ref[...], c))
    out = tz * lax.rsqrt(jnp.mean(tz * tz, axis=-1, keepdims=True) + EPS) * nw_ref[...]
    o_ref[...] = out[:rows]

    expand_t = ((_iota((SSM_INNER, LANE), 0) >> 6) == _iota((SSM_INNER, LANE), 1)).astype(BF16)
    cum_last = jnp.broadcast_to(cum_t[:, tv - 1:tv], (LANE, LANE))
    chunk_decay = jnp.exp(_dot_split_rhs(expand_t, cum_last))
    contrib = jnp.concatenate([_dot_tn(x_to_end[:, half * g:half * (g + 1)], bb) for g in range(SSM_GROUPS)], axis=0)
    h_new = chunk_decay * hs + contrib
    h_ref[...] = h_new

    @pl.when(n == pl.num_programs(1) - 1)
    def _():
        hn_ref[...] = h_new
        convn_ref[...] = tail


def _ssd(proj, conv0, h0, lp):
    b, t, _ = proj.shape
    cb = min(t, CHUNK)
    kern = functools.partial(_ssd_kernel, c=CHUNK, tv=cb)
    const = lambda i, n: (0, 0)
    return pl.pallas_call(
        kern,
        grid=(b, t // cb),
        in_specs=[
            pl.BlockSpec((None, cb, SSM_INNER), lambda i, n: (i, n, OFF_Z // SSM_INNER)),
            pl.BlockSpec((None, cb, LANE), lambda i, n: (i, n, OFF_DT // LANE)),
            pl.BlockSpec((None, cb, SSM_CONV_DIM), lambda i, n: (i, n, OFF_XBC // SSM_CONV_DIM)),
            pl.BlockSpec((None, SUBLANE, SSM_CONV_DIM), lambda i, n: (i, 0, 0)),
            pl.BlockSpec((None, SSM_INNER, LANE), lambda i, n: (i, 0, 0)),
            pl.BlockSpec((SSM_CONV, SSM_CONV_DIM), const),
            pl.BlockSpec((1, SSM_CONV_DIM), const),
            pl.BlockSpec((1, LANE), const),
            pl.BlockSpec((1, LANE), const),
            pl.BlockSpec((1, SSM_INNER), const),
            pl.BlockSpec((1, SSM_INNER), const),
        ],
        out_specs=[
            pl.BlockSpec((None, cb, SSM_INNER), lambda i, n: (i, n, 0)),
            pl.BlockSpec((None, SUBLANE, SSM_CONV_DIM), lambda i, n: (i, 0, 0)),
            pl.BlockSpec((None, SSM_INNER, LANE), lambda i, n: (i, 0, 0)),
        ],
        out_shape=[
            jax.ShapeDtypeStruct((b, t, SSM_INNER), F32),
            jax.ShapeDtypeStruct((b, SUBLANE, SSM_CONV_DIM), F32),
            jax.ShapeDtypeStruct((b, SSM_INNER, LANE), F32),
        ],
        scratch_shapes=[pltpu.VMEM((CHUNK + SUBLANE, SSM_CONV_DIM), F32), pltpu.VMEM((SSM_INNER, LANE), F32)],
        compiler_params=_params("parallel", "arbitrary"),
    )(proj, proj, proj, conv0, h0, lp["ssm_conv_w"], lp["ssm_conv_b"], lp["ssm_dt_bias"], lp["ssm_a_log"],
      lp["ssm_d"], lp["ssm_norm"])


def _sb_tile(z, earlier, carry, upper):
    ls = _log_sigmoid(z)
    lk = ls - z
    if earlier is not None:
        lk = jnp.where(earlier, lk, 0.0)
    within = _dot_split_lhs(lk, upper, 2)
    a = jnp.exp(ls + within + carry)
    if earlier is not None:
        a = jnp.where(earlier, a, 0.0)
    return a, carry + jnp.sum(lk, axis=1, keepdims=True)


def _upper_ones(tk):
    return (_iota((tk, tk), 0) > _iota((tk, tk), 1)).astype(BF16)


def _all_done(carry):
    c = carry if not isinstance(carry, (list, tuple)) else functools.reduce(jnp.maximum, carry)
    return jnp.max(c, axis=0, keepdims=True)[0, 0] <= -SB_DONE


def _sb_prompt_kernel(q_ref, k_ref, v_ref, o_ref, acc_ref, carry_ref):
    i = pl.program_id(2)
    tq = q_ref.shape[0]
    tk = tq
    acc_ref[...] = jnp.zeros(acc_ref.shape, F32)
    carry_ref[...] = jnp.zeros(carry_ref.shape, F32)
    q2 = q_ref[...] * (ATT_DH ** -0.5)
    lane = _iota((tq, LANE), 1)
    qm = [jnp.where((lane >= ATT_DH) == (hh == 1), q2, 0.0).astype(BF16) for hh in range(2)]
    upper = _upper_ones(tk)

    def tile(j, earlier):
        off = pl.multiple_of(j * tk, tk)
        k2 = k_ref[pl.ds(off, tk), :].astype(BF16)
        v2 = v_ref[pl.ds(off, tk), :].astype(BF16)
        for hh in range(2):
            a, carry = _sb_tile(_dot_nt(qm[hh], k2), earlier, carry_ref[hh], upper)
            acc_ref[hh] += _dot(a.astype(BF16), v2)
            carry_ref[hh] = carry
        return _all_done([carry_ref[0], carry_ref[1]])

    done = tile(i, _iota((tq, tk), 1) < _iota((tq, tk), 0))

    def body(state):
        j, _ = state
        return j - 1, tile(j, None).astype(jnp.int32)

    lax.while_loop(lambda st: (st[0] >= 0) & (st[1] == 0), body, (i - 1, done.astype(jnp.int32)))
    o_ref[...] = jnp.where(lane < ATT_DH, acc_ref[0], acc_ref[1])


def _sb_prompt(proj):
    b, t, _ = proj.shape
    tq = min(ATT_TILE, t)
    return pl.pallas_call(
        _sb_prompt_kernel,
        grid=(b, ATT_HEADS // 2, t // tq),
        in_specs=[
            pl.BlockSpec((None, tq, LANE), lambda i, p, n: (i, n, OFF_SBQ // LANE + p)),
            pl.BlockSpec((None, t, LANE), lambda i, p, n: (i, 0, OFF_SBK // LANE + p)),
            pl.BlockSpec((None, t, LANE), lambda i, p, n: (i, 0, OFF_SBV // LANE + p)),
        ],
        out_specs=pl.BlockSpec((None, tq, LANE), lambda i, p, n: (i, n, p)),
        out_shape=jax.ShapeDtypeStruct((b, t, ATT_HEADS * ATT_DH), F32),
        scratch_shapes=[pltpu.VMEM((2, tq, LANE), F32), pltpu.VMEM((2, tq, 1), F32)],
        compiler_params=_params("parallel", "parallel", "arbitrary"),
    )(proj, proj, proj)


def _sb_sample_kernel(pt_ref, l_ref, q_ref, kn_ref, vn_ref, acc0_ref, carry0_ref, kp_ref, vp_ref, o_ref, co_ref,
                      acc_ref, carry_ref, *, with_new):
    jj = pl.program_id(1)
    nq = q_ref.shape[0] // ATT_HEADS
    upper = _upper_ones(PAGE_SIZE)
    qf = q_ref[...] * (ATT_DH ** -0.5)
    qh = [qf[nq * h:nq * (h + 1)].astype(BF16) for h in range(ATT_HEADS)]

    def visit(kt_ref, vt_ref, earlier):
        z = jnp.concatenate([_dot(qh[h], kt_ref[h].astype(BF16)) for h in range(ATT_HEADS)], axis=0)
        a, carry = _sb_tile(z, earlier, carry_ref[...], upper)
        acc_ref[...] += jnp.concatenate(
            [_dot_nt(a[nq * h:nq * (h + 1)].astype(BF16), vt_ref[h].astype(BF16)) for h in range(ATT_HEADS)], axis=0)
        carry_ref[...] = carry

    @pl.when(jj == 0)
    def _():
        acc_ref[...] = acc0_ref[...]
        carry_ref[...] = carry0_ref[...]
        if with_new:
            rows = _iota((ATT_HEADS * nq, PAGE_SIZE), 0) & (nq - 1)
            visit(kn_ref, vn_ref, _iota((ATT_HEADS * nq, PAGE_SIZE), 1) < rows)

    @pl.when(jnp.logical_not(_all_done(carry_ref[...])))
    def _():
        visit(kp_ref, vp_ref, None)

    @pl.when(jj == pl.num_programs(1) - 1)
    def _():
        o_ref[...] = acc_ref[...]
        co_ref[...] = carry_ref[...]


def _sb_sample_pages(q, kn_t, vn_t, acc0, carry0, cache_k, cache_v, page_table, layer, first_page, n_visit, with_new):
    b, rows, _ = q.shape
    page = lambda i, jj, pt, l: (l[0], pt[i, first_page - jj], 0, 0, 0)
    per_b3 = lambda i, jj, pt, l: (i, 0, 0)
    per_b4 = lambda i, jj, pt, l: (i, 0, 0, 0)
    kv_t = (None, ATT_HEADS, ATT_DH, PAGE_SIZE)
    grid_spec = pltpu.PrefetchScalarGridSpec(
        num_scalar_prefetch=2,
        grid=(b, n_visit),
        in_specs=[
            pl.BlockSpec((None, rows, ATT_DH), per_b3),
            pl.BlockSpec(kv_t, per_b4),
            pl.BlockSpec(kv_t, per_b4),
            pl.BlockSpec((None, rows, ATT_DH), per_b3),
            pl.BlockSpec((None, rows, 1), per_b3),
            pl.BlockSpec((None,) + kv_t, page),
            pl.BlockSpec((None,) + kv_t, page),
        ],
        out_specs=[pl.BlockSpec((None, rows, ATT_DH), per_b3), pl.BlockSpec((None, rows, 1), per_b3)],
        scratch_shapes=[pltpu.VMEM((rows, ATT_DH), F32), pltpu.VMEM((rows, 1), F32)],
    )
    return pl.pallas_call(
        functools.partial(_sb_sample_kernel, with_new=with_new),
        grid_spec=grid_spec,
        out_shape=[jax.ShapeDtypeStruct((b, rows, ATT_DH), F32), jax.ShapeDtypeStruct((b, rows, 1), F32)],
        compiler_params=_params("parallel", "arbitrary"),
    )(page_table, layer, q, kn_t, vn_t, acc0, carry0, cache_k, cache_v)


def _sb_sample(q, kn_t, vn_t, cache_k, cache_v, page_table, layer):
    b, rows, _ = q.shape
    n_pages = page_table.shape[1]
    first = min(SB_FIRST_PAGES, n_pages)
    acc, carry = _sb_sample_pages(q, kn_t, vn_t, jnp.zeros((b, rows, ATT_DH), F32), jnp.zeros((b, rows, 1), F32),
                                  cache_k, cache_v, page_table, layer, n_pages - 1, first, True)
    if first == n_pages:
        return acc
    older = lambda args: _sb_sample_pages(q, kn_t, vn_t, args[0], args[1], cache_k, cache_v, page_table, layer,
                                          n_pages - 1 - first, n_pages - first, False)[0]
    return lax.cond(jnp.max(carry) > -SB_DONE, older, lambda args: args[0], (acc, carry))


def _select_bias(scores, own, axis=1):
    blk = _iota(scores.shape, axis)
    s = jnp.where(blk < own, scores, -jnp.inf)
    keep = blk == own
    for _ in range(MOBA_TOPK):
        m = jnp.max(s, axis=axis, keepdims=True)
        is_max = (s == m) & (m > -jnp.inf)
        first = jnp.min(jnp.where(is_max, blk, LANE), axis=axis, keepdims=True)
        pick = blk == first
        keep = keep | pick
        s = jnp.where(pick, -jnp.inf, s)
    return jnp.where(keep, 0.0, NEG_BIAS)


def _kmean_prompt_kernel(k_ref, o_ref):
    o_ref[...] = jnp.mean(k_ref[...], axis=0, keepdims=True)


def _kmean_prompt(proj):
    b, t, _ = proj.shape
    nb = t // MOBA_BLOCK
    w = ATT_HEADS * ATT_DH
    return pl.pallas_call(
        _kmean_prompt_kernel,
        grid=(b, nb),
        in_specs=[pl.BlockSpec((None, MOBA_BLOCK, w), lambda i, n: (i, n, OFF_MBK // w))],
        out_specs=pl.BlockSpec((None, None, 1, w), lambda i, n: (i, n, 0, 0)),
        out_shape=jax.ShapeDtypeStruct((b, nb, 1, w), F32),
        compiler_params=_params("parallel", "parallel"),
    )(proj)


def _select_prompt_kernel(q_ref, km_ref, o_ref):
    i = pl.program_id(2)
    tq = q_ref.shape[0]
    q = q_ref[...]
    km = km_ref[...]
    lane = _iota((tq, LANE), 1)
    own = (i * tq + _iota((1, tq), 1)) >> 8
    for hh in range(2):
        qm = jnp.where((lane >= ATT_DH) == (hh == 1), q, 0.0)
        scores_t = lax.dot_general(km, qm, NT_DIMS, precision=lax.Precision.HIGHEST, preferred_element_type=F32)
        o_ref[hh] = _select_bias(scores_t, own, axis=0)


def _select_prompt(proj, kmean):
    b, t, _ = proj.shape
    tq = min(ATT_TILE, t)
    return pl.pallas_call(
        _select_prompt_kernel,
        grid=(b, ATT_HEADS // 2, t // tq),
        in_specs=[
            pl.BlockSpec((None, tq, LANE), lambda i, p, n: (i, n, OFF_MBQ // LANE + p)),
            pl.BlockSpec((None, LANE, LANE), lambda i, p, n: (i, 0, p)),
        ],
        out_specs=pl.BlockSpec((None, 2, LANE, tq), lambda i, p, n: (i, p, 0, n)),
        out_shape=jax.ShapeDtypeStruct((b, ATT_HEADS, LANE, t), F32),
        compiler_params=_params("parallel", "parallel", "parallel"),
    )(proj, kmean)


def _moba_prompt_kernel(q_ref, k_ref, vt_ref, bias_ref, o_ref, acc_ref, m_ref, sa_ref, sb_ref):
    i = pl.program_id(2)
    tq = q_ref.shape[0]
    tk = tq
    acc_ref[...] = jnp.zeros(acc_ref.shape, F32)
    m_ref[...] = jnp.full(m_ref.shape, -jnp.inf, F32)
    q2 = q_ref[...] * (ATT_DH ** -0.5)
    lane = _iota((tq, LANE), 1)
    qm = [jnp.where((lane >= ATT_DH) == (hh == 1), q2, 0.0).astype(BF16) for hh in range(2)]
    first_rows = _iota((LANE, tk), 0) < ATT_DH
    head_rows = [first_rows, jnp.logical_not(first_rows)]

    def scores(j, dst_ref):
        jc = jnp.maximum(j, 0)
        k2 = k_ref[pl.ds(pl.multiple_of(jc * tk, tk), tk), :].astype(BF16)
        gate = jnp.where(j >= 0, 0.0, -jnp.inf)
        for hh in range(2):
            dst_ref[hh] = _dot_nt(k2, qm[hh]) + (bias_ref[hh, pl.ds(jc, 1), :] + gate)

    def accumulate(j, src_ref, visible):
        vt = vt_ref[:, pl.ds(pl.multiple_of(jnp.maximum(j, 0) * tk, tk), tk)]
        for hh in range(2):
            vt_h = jnp.where(head_rows[hh], vt, 1.0).astype(BF16)
            s = src_ref[hh]
            if visible is not None:
                s = jnp.where(visible, s, -jnp.inf)
            m_prev = m_ref[hh]
            m_new = jnp.maximum(m_prev, jnp.max(s, axis=0, keepdims=True))
            p = jnp.exp(s - m_new)
            acc_ref[hh] = jnp.exp(m_prev - m_new) * acc_ref[hh] + _dot(vt_h, p.astype(BF16))
            m_ref[hh] = m_new

    scores(i, sa_ref)
    scores(i - 1, sb_ref)
    accumulate(i, sa_ref, _iota((tk, tq), 0) <= _iota((tk, tq), 1))

    def body(u, c):
        j = i - 1 - 2 * u
        accumulate(j, sb_ref, None)
        scores(j - 1, sa_ref)
        accumulate(j - 1, sa_ref, None)
        scores(j - 2, sb_ref)
        return c

    lax.fori_loop(0, (i + 1) // 2, body, 0)
    a0, a1 = acc_ref[0], acc_ref[1]
    o_t = jnp.concatenate([a0[:ATT_DH] / a0[ATT_DH:], a1[ATT_DH:] / a1[:ATT_DH]], axis=0)
    o_ref[...] = o_t.T


def _moba_prompt(proj, v_t, bias):
    b, t, _ = proj.shape
    tq = min(ATT_TILE, t)
    assert tq == MOBA_BLOCK, "the prompt MoBA kernel visits one key block per tile"
    return pl.pallas_call(
        _moba_prompt_kernel,
        grid=(b, ATT_HEADS // 2, t // tq),
        in_specs=[
            pl.BlockSpec((None, tq, LANE), lambda i, p, n: (i, n, OFF_MBQ // LANE + p)),
            pl.BlockSpec((None, t, LANE), lambda i, p, n: (i, 0, OFF_MBK // LANE + p)),
            pl.BlockSpec((None, LANE, t), lambda i, p, n: (i, p, 0)),
            pl.BlockSpec((None, 2, LANE, tq), lambda i, p, n: (i, p, 0, n)),
        ],
        out_specs=pl.BlockSpec((None, tq, LANE), lambda i, p, n: (i, n, p)),
        out_shape=jax.ShapeDtypeStruct((b, t, ATT_HEADS * ATT_DH), F32),
        scratch_shapes=[pltpu.VMEM((2, LANE, tq), F32), pltpu.VMEM((2, 1, tq), F32),
                        pltpu.VMEM((2, tq, tq), F32), pltpu.VMEM((2, tq, tq), F32)],
        compiler_params=_params("parallel", "parallel", "arbitrary"),
    )(proj, proj, v_t, bias)


def _put_lane(ref, n, col):
    ref[...] = jnp.where(_iota(ref.shape, 1) == n, col, ref[...])


def _moba_sample_kernel(pt_ref, l_ref, q_ref, kn_ref, vn_ref, *refs, own, bps):
    k_refs, v_refs = refs[:2 * bps], refs[2 * bps:4 * bps]
    o_ref, score_ref, m_ref, l_blk_ref, o_blk_ref, own_m_ref, own_l_ref, own_o_ref = refs[4 * bps:]
    step = pl.program_id(1)
    nq = q_ref.shape[0] // ATT_HEADS
    rows = ATT_HEADS * nq
    qf = q_ref[...]
    qh = [qf[nq * h:nq * (h + 1)].astype(BF16) for h in range(ATT_HEADS)]
    scale = ATT_DH ** -0.5

    def logits(kt_refs):
        return jnp.concatenate(
            [jnp.concatenate([_dot(qh[h], kt[h].astype(BF16)) for kt in kt_refs], axis=1)
             for h in range(ATT_HEADS)], axis=0)

    def partials(s, vt_refs):
        m = jnp.max(s, axis=1, keepdims=True)
        p = jnp.exp(s - m)
        o = None
        for c, vt in enumerate(vt_refs):
            oc = jnp.concatenate(
                [_dot_nt(p[nq * h:nq * (h + 1), PAGE_SIZE * c:PAGE_SIZE * (c + 1)].astype(BF16), vt[h].astype(BF16))
                 for h in range(ATT_HEADS)], axis=0)
            o = oc if o is None else o + oc
        return m, jnp.sum(p, axis=1, keepdims=True), o

    @pl.when(step == 0)
    def _():
        score_ref[...] = jnp.zeros(score_ref.shape, F32)
        m_ref[...] = jnp.zeros(m_ref.shape, F32)
        l_blk_ref[...] = jnp.zeros(l_blk_ref.shape, F32)
        qrow = _iota((rows, PAGE_SIZE), 0) & (nq - 1)
        s = jnp.where(_iota((rows, PAGE_SIZE), 1) <= qrow, logits([kn_ref]) * scale, -jnp.inf)
        own_m_ref[...], own_l_ref[...], own_o_ref[...] = partials(s, [vn_ref])

    for c in range(bps):
        n = step * bps + c
        z = logits(k_refs[2 * c:2 * c + 2])
        _put_lane(score_ref, n, jnp.sum(z, axis=1, keepdims=True) * (1.0 / MOBA_BLOCK))
        m, l, o = partials(z * scale, v_refs[2 * c:2 * c + 2])
        _put_lane(m_ref, n, m)
        _put_lane(l_blk_ref, n, l)
        o_blk_ref[n] = o

    @pl.when(step == pl.num_programs(1) - 1)
    def _():
        keep = _select_bias(score_ref[...], jnp.full((rows, 1), own, jnp.int32)) == 0.0
        keep = keep & (_iota((rows, LANE), 1) < own)
        m_own, l_own = own_m_ref[...], own_l_ref[...]
        m_blk = m_ref[...]
        m_all = jnp.maximum(m_own, jnp.max(jnp.where(keep, m_blk, -jnp.inf), axis=1, keepdims=True))
        w = jnp.where(keep, jnp.exp(m_blk - m_all), 0.0)
        w_own = jnp.exp(m_own - m_all)
        l_all = w_own * l_own + jnp.sum(w * l_blk_ref[...], axis=1, keepdims=True)
        acc = w_own * own_o_ref[...]
        for blk in range(own):
            acc = acc + w[:, blk:blk + 1] * o_blk_ref[blk]
        o_ref[...] = acc / l_all


def _moba_sample(q, kn_t, vn_t, cache_k, cache_v, page_table, layer):
    b, rows, _ = q.shape
    n_pages = page_table.shape[1]
    ppb = MOBA_BLOCK // PAGE_SIZE
    assert ppb == 2 and n_pages % ppb == 0 and n_pages // ppb < LANE
    n_blocks = n_pages // ppb
    bps = MOBA_SAMPLE_BLOCKS_PER_STEP if n_blocks % MOBA_SAMPLE_BLOCKS_PER_STEP == 0 else 1
    pps = ppb * bps
    page = lambda c: (lambda i, n, pt, l: (l[0], pt[i, pps * n + c], 0, 0, 0))
    per_b3 = lambda i, n, pt, l: (i, 0, 0)
    per_b4 = lambda i, n, pt, l: (i, 0, 0, 0)
    kv_t = (None, ATT_HEADS, ATT_DH, PAGE_SIZE)
    pages = [pl.BlockSpec((None,) + kv_t, page(c)) for c in range(pps)]
    grid_spec = pltpu.PrefetchScalarGridSpec(
        num_scalar_prefetch=2,
        grid=(b, n_blocks // bps),
        in_specs=[pl.BlockSpec((None, rows, ATT_DH), per_b3), pl.BlockSpec(kv_t, per_b4), pl.BlockSpec(kv_t, per_b4)]
        + pages + pages,
        out_specs=pl.BlockSpec((None, rows, ATT_DH), per_b3),
        scratch_shapes=[pltpu.VMEM((rows, LANE), F32), pltpu.VMEM((rows, LANE), F32), pltpu.VMEM((rows, LANE), F32),
                        pltpu.VMEM((n_blocks, rows, ATT_DH), F32), pltpu.VMEM((rows, 1), F32),
                        pltpu.VMEM((rows, 1), F32), pltpu.VMEM((rows, ATT_DH), F32)],
    )
    return pl.pallas_call(
        functools.partial(_moba_sample_kernel, own=n_blocks, bps=bps),
        grid_spec=grid_spec,
        out_shape=jax.ShapeDtypeStruct((b, rows, ATT_DH), F32),
        compiler_params=_params("parallel", "arbitrary"),
    )(page_table, layer, q, kn_t, vn_t, *([cache_k] * pps), *([cache_v] * pps))


def _merge_kernel(o0_ref, o1_ref, o2_ref, o3_ref, g0_ref, g1_ref, g2_ref, g3_ref, x_ref, wb_ref, wo_ref, nw_ref,
                  x1_ref, hf_ref):
    mixed = None
    for o_ref, g_ref, i in ((o0_ref, g0_ref, 0), (o1_ref, g1_ref, 1), (o2_ref, g2_ref, 2), (o3_ref, g3_ref, 3)):
        term = _sigmoid(g_ref[...]) * _dot(o_ref[...].astype(BF16), wb_ref[i])
        mixed = term if mixed is None else mixed + term
    x1 = x_ref[...] + _dot(mixed.astype(BF16), wo_ref[...])
    x1_ref[...] = x1
    ms = jnp.mean(x1 * x1, axis=-1, keepdims=True)
    hf_ref[...] = (x1 * lax.rsqrt(ms + EPS) * nw_ref[...]).astype(BF16)


def _merge(branches, proj, x, lp):
    n, d = x.shape
    tm = _tile(n, 256)
    row = lambda i: (i, 0)
    gate = lambda g: (lambda i: (i, OFF_GATE // d + g))
    return pl.pallas_call(
        _merge_kernel,
        grid=(n // tm,),
        in_specs=[pl.BlockSpec((tm, BRANCH_WIDTH), row)] * 4
        + [pl.BlockSpec((tm, d), gate(g)) for g in range(4)]
        + [
            pl.BlockSpec((tm, d), row),
            pl.BlockSpec((4, BRANCH_WIDTH, d), lambda i: (0, 0, 0)),
            pl.BlockSpec((d, d), lambda i: (0, 0)),
            pl.BlockSpec((1, d), lambda i: (0, 0)),
        ],
        out_specs=[pl.BlockSpec((tm, d), row), pl.BlockSpec((tm, d), row)],
        out_shape=[jax.ShapeDtypeStruct((n, d), F32), jax.ShapeDtypeStruct((n, d), BF16)],
        compiler_params=_params("parallel"),
    )(*branches, proj, proj, proj, proj, x, lp["w_branch"], lp["w_out"], lp["norm_ffn"])


def _ffn_kernel(u_ref, halo_ref, halo0_ref, x1_ref, cw_ref, cb_ref, wd_ref, o_ref, ext_ref, *, tiles_per_seq):
    i = pl.program_id(0)
    tm = u_ref.shape[0]
    seq_start = (i % tiles_per_seq) == 0

    @pl.when(seq_start)
    def _():
        ext_ref[0:SUBLANE, :] = halo0_ref[...]

    @pl.when(jnp.logical_not(seq_start))
    def _():
        ext_ref[0:SUBLANE, :] = halo_ref[...]

    u = u_ref[...]
    ext_ref[SUBLANE:SUBLANE + tm, :] = u
    cw = cw_ref[...]
    conv = u * cw[FFN_CONV - 1:FFN_CONV] + cb_ref[...]
    for t in range(FFN_CONV - 1):
        off = SUBLANE - (FFN_CONV - 1) + t
        conv = conv + ext_ref[off:off + tm, :] * cw[t:t + 1]
    act = _silu(conv[:, :D_FF]) * conv[:, D_FF:]
    o_ref[...] = x1_ref[...] + _dot(act.astype(BF16), wd_ref[...])


def _ffn(u, halo0, x1, seq_len, lp):
    n, w = u.shape
    d = x1.shape[1]
    tm = _tile(seq_len, 256)
    tiles_per_seq = seq_len // tm
    hb = tm // SUBLANE
    return pl.pallas_call(
        functools.partial(_ffn_kernel, tiles_per_seq=tiles_per_seq),
        grid=(n // tm,),
        in_specs=[
            pl.BlockSpec((tm, w), lambda i: (i, 0)),
            pl.BlockSpec((SUBLANE, w), lambda i: (jnp.maximum(i * hb - 1, 0), 0)),
            pl.BlockSpec((None, SUBLANE, w), lambda i: (i // tiles_per_seq, 0, 0)),
            pl.BlockSpec((tm, d), lambda i: (i, 0)),
            pl.BlockSpec((FFN_CONV, w), lambda i: (0, 0)),
            pl.BlockSpec((1, w), lambda i: (0, 0)),
            pl.BlockSpec((D_FF, d), lambda i: (0, 0)),
        ],
        out_specs=pl.BlockSpec((tm, d), lambda i: (i, 0)),
        out_shape=jax.ShapeDtypeStruct((n, d), F32),
        scratch_shapes=[pltpu.VMEM((tm + SUBLANE, w), F32)],
        compiler_params=_params("parallel"),
    )(u, u, halo0, x1, lp["ffn_conv_w"], lp["ffn_conv_b"], lp["w_down"])


def _rope_tables(pos0, t):
    half = RET_DK // 2
    inv = ROPE_BASE ** (-jnp.arange(half, dtype=F32) / half)
    ang = (pos0 + jnp.arange(t, dtype=jnp.int32)).astype(F32)[:, None] * inv[None, :]
    cos, sin = jnp.cos(ang), jnp.sin(ang)
    cos_t = jnp.tile(jnp.concatenate([cos, cos], axis=1), (1, RET_HEADS))
    sin_t = jnp.tile(jnp.concatenate([-sin, sin], axis=1), (1, RET_HEADS))
    return cos_t, sin_t


def _pad_state_rows(s):
    return jnp.pad(s, ((0, 0), (SUBLANE - s.shape[1], 0), (0, 0)))


def _ssm_state_in(h0):
    b = h0.shape[0]
    h2 = h0.reshape(b, SSM_INNER, SSM_STATE)
    g0 = (jnp.arange(SSM_INNER) < SSM_INNER // SSM_GROUPS)[None, :, None]
    return jnp.concatenate([jnp.where(g0, h2, 0.0), jnp.where(g0, 0.0, h2)], axis=-1)


def _ssm_state_out(h2):
    b = h2.shape[0]
    g0 = (jnp.arange(SSM_INNER) < SSM_INNER // SSM_GROUPS)[None, :, None]
    return jnp.where(g0, h2[..., :SSM_STATE], h2[..., SSM_STATE:]).reshape(b, SSM_HEADS, SSM_HEADDIM, SSM_STATE)


def _prep_layer_weights(w):
    depth = w["w_in"].shape[0]
    w_in = w["w_in"]
    e_xbc = OFF_Z + SSM_INNER + SSM_CONV_DIM
    w_in = jnp.concatenate([
        w_in[..., :OFF_Z + SSM_INNER],
        w_in[..., e_xbc:e_xbc + SSM_HEADS],
        jnp.zeros(w_in.shape[:2] + (DT_PAD - SSM_HEADS,), w_in.dtype),
        w_in[..., OFF_Z + SSM_INNER:e_xbc],
        w_in[..., e_xbc + SSM_HEADS:],
    ], axis=-1)
    assert w_in.shape[-1] == P_TOTAL
    pad_heads = lambda a: jnp.pad(a, ((0, 0), (0, LANE - SSM_HEADS)))[:, None, :]
    return {
        "norm_mix": w["norm_mix"],
        "w_in": w_in.astype(BF16),
        "ssm_conv_w": w["ssm_conv_w"],
        "ssm_conv_b": w["ssm_conv_b"][:, None, :],
        "ssm_dt_bias": pad_heads(w["ssm_dt_bias"]),
        "ssm_a_log": pad_heads(w["ssm_a_log"]),
        "ssm_d": jnp.repeat(w["ssm_d"], SSM_HEADDIM, axis=1)[:, None, :],
        "ssm_norm": w["ssm_norm"][:, None, :],
        "w_branch": w["w_branch"].astype(BF16),
        "w_out": w["w_out"].astype(BF16),
        "norm_ffn": w["norm_ffn"][:, None, :],
        "w_up": w["w_up"].astype(BF16),
        "ffn_conv_w": w["ffn_conv_w"],
        "ffn_conv_b": w["ffn_conv_b"][:, None, :],
        "w_down": w["w_down"].astype(BF16),
        "layer": jnp.arange(depth, dtype=jnp.int32)[:, None],
    }


def _heads_major(a, b, t):
    return a.reshape(b, t, ATT_HEADS, ATT_DH).transpose(0, 2, 1, 3)


def _layer(x, lp, past, pos0, caches, page_table):
    b, t, d = x.shape
    n = b * t
    r0, h0, conv0, ffn0 = past
    x2d = x.reshape(n, d)
    hn = _rmsnorm(x2d, lp["norm_mix"], BF16)
    proj = _matmul(hn, lp["w_in"])
    proj3 = proj.reshape(b, t, P_TOTAL)
    w_att = ATT_HEADS * ATT_DH

    cos, sin = _rope_tables(pos0, t)
    o_ret, r_new = _retention(proj3, cos, sin, r0)
    o_ssm, conv_new, h_new = _ssd(proj3, _pad_state_rows(conv0), _ssm_state_in(h0), lp)

    sbk, sbv = proj[:, OFF_SBK:OFF_SBK + w_att], proj[:, OFF_SBV:OFF_SBV + w_att]
    mbk, mbv = proj[:, OFF_MBK:OFF_MBK + w_att], proj[:, OFF_MBV:OFF_MBV + w_att]
    if caches is None:
        o_sb = _sb_prompt(proj3)
        kmean = _kmean_prompt(proj3)[:, :, 0, :]
        kmean = jnp.pad(kmean, ((0, 0), (0, LANE - kmean.shape[1]), (0, 0)))
        v_t = proj3[:, :, OFF_MBV:OFF_MBV + w_att].transpose(0, 2, 1)
        o_mb = _moba_prompt(proj3, v_t, _select_prompt(proj3, kmean))
    else:
        csk, csv, cmk, cmv = caches
        layer = lp["layer"]
        rows = ATT_HEADS * t
        new_t = lambda a: jnp.pad(a.reshape(b, t, ATT_HEADS, ATT_DH).transpose(0, 2, 3, 1),
                                  ((0, 0), (0, 0), (0, 0), (0, PAGE_SIZE - t)))
        to_rows = lambda a: _heads_major(a, b, t).reshape(b, rows, ATT_DH)
        from_rows = lambda o: o.reshape(b, ATT_HEADS, t, ATT_DH).transpose(0, 2, 1, 3).reshape(b, t, w_att)
        sq = to_rows(proj[:, OFF_SBQ:OFF_SBQ + w_att])
        o_sb = from_rows(_sb_sample(sq, new_t(sbk), new_t(sbv), csk, csv, page_table, layer))
        mq = to_rows(proj[:, OFF_MBQ:OFF_MBQ + w_att])
        o_mb = from_rows(_moba_sample(mq, new_t(mbk), new_t(mbv), cmk, cmv, page_table, layer))

    branches = [o.reshape(n, BRANCH_WIDTH) for o in (o_ret, o_sb, o_mb, o_ssm)]
    x1, hf = _merge(branches, proj, x2d, lp)
    u = _matmul(hf, lp["w_up"])
    x2 = _ffn(u, _pad_state_rows(ffn0), x1, t, lp)
    heads = lambda a: a.reshape(b, t, ATT_HEADS, ATT_DH)
    state = (heads(sbk), heads(sbv), heads(mbk), heads(mbv), r_new, _ssm_state_out(h_new),
             conv_new[:, SUBLANE - (SSM_CONV - 1):], u.reshape(b, t, -1)[:, t - (FFN_CONV - 1):])
    return x2.reshape(b, t, d), state


def _trunk(x, pos0, past, lw, norm_final, caches, page_table):
    def body(carry, per_layer):
        lp, pst = per_layer
        y, st = _layer(carry, lp, pst, pos0, caches, page_table)
        return y, st

    y, states = lax.scan(body, x, (lw, past))
    b, t, d = y.shape
    out = _rmsnorm(y.reshape(b * t, d), norm_final, F32).reshape(b, t, d)
    return out, states


def kernel(x_prompt, x_sample, cache_sb_k, cache_sb_v, cache_moba_k, cache_moba_v, page_table, state_ret, state_ssm,
           state_ssm_conv, state_ffn_conv, norm_mix, w_in, ssm_conv_w, ssm_conv_b, ssm_dt_bias, ssm_a_log, ssm_d,
           ssm_norm, w_branch, w_out, norm_ffn, w_up, ffn_conv_w, ffn_conv_b, w_down, norm_final):
    lw = _prep_layer_weights({
        "norm_mix": norm_mix, "w_in": w_in, "ssm_conv_w": ssm_conv_w, "ssm_conv_b": ssm_conv_b,
        "ssm_dt_bias": ssm_dt_bias, "ssm_a_log": ssm_a_log, "ssm_d": ssm_d, "ssm_norm": ssm_norm,
        "w_branch": w_branch, "w_out": w_out, "norm_ffn": norm_ffn, "w_up": w_up, "ffn_conv_w": ffn_conv_w,
        "ffn_conv_b": ffn_conv_b, "w_down": w_down})
    depth = w_in.shape[0]
    bp = x_prompt.shape[0]
    zeros = lambda *s: jnp.zeros((depth, bp) + s, F32)
    past_p = (zeros(RET_HEADS, RET_DK, RET_DV), zeros(SSM_HEADS, SSM_HEADDIM, SSM_STATE),
              zeros(SSM_CONV - 1, SSM_CONV_DIM), zeros(FFN_CONV - 1, 2 * D_FF))
    y_p, st_p = _trunk(x_prompt, 0, past_p, lw, norm_final, None, None)

    caches = tuple(c.transpose(0, 1, 3, 4, 2) for c in (cache_sb_k, cache_sb_v, cache_moba_k, cache_moba_v))
    past_s = (state_ret, state_ssm, state_ssm_conv, state_ffn_conv)
    pos0 = page_table.shape[1] * PAGE_SIZE
    y_s, st_s = _trunk(x_sample, pos0, past_s, lw, norm_final, caches, page_table)
    return (y_p, y_s) + tuple(st_p) + tuple(st_s)
```
